```python
import jax, jax.numpy as jnp
from jax import lax
import numpy as np

D_MODEL = 2048
BATCH = 4
SEQ = 4096
DEPTH = 2

N_HEADS = D_MODEL // 128
HEAD_DIM = 64
N_KV_HEADS = N_HEADS // 4
GQA_GROUP = N_HEADS // N_KV_HEADS
D_ATTN = N_HEADS * HEAD_DIM
D_KV = N_KV_HEADS * HEAD_DIM
WINDOW = 128
BLOCK = 128
D_CONV = D_MODEL // 2
CONV_WIDTH = 3
D_FF = ((8 * D_MODEL // 3 + 255) // 256) * 256
N_EXPERTS = 8
TOP_K = 2
MOE_D_FF = 7 * D_MODEL // 2
N_DENSE = (DEPTH + 1) // 2
N_MOE = DEPTH // 2
N_MOD = 6
EPS = 1e-6

IN_WIDTHS = [D_ATTN, D_KV, D_KV, D_CONV, D_CONV, D_CONV, D_MODEL, D_MODEL]
IN_SPLITS = [int(s) for s in np.cumsum(IN_WIDTHS)[:-1]]
D_IN = int(sum(IN_WIDTHS))

kernel_name = "hybrid_swa_shortconv_moe_adaln"


def rms_norm(x, gain):
    xf = x.astype(jnp.float32)
    y = xf * lax.rsqrt(jnp.mean(xf * xf, axis=-1, keepdims=True) + EPS)
    return (y * gain.astype(jnp.float32)).astype(x.dtype)


def alibi_slopes(n_heads):
    return 2.0 ** (-8.0 * jnp.arange(1, n_heads + 1, dtype=jnp.float32) / n_heads)


def sliding_window_attention(q, k, v, sinks):
    b, s = q.shape[0], q.shape[1]
    nb = s // BLOCK
    qb = q.reshape(b, nb, BLOCK, N_KV_HEADS, GQA_GROUP, HEAD_DIM)

    def band(t):
        tb = t.reshape(b, nb, BLOCK, N_KV_HEADS, HEAD_DIM)
        prev = jnp.pad(tb[:, :-1], ((0, 0), (1, 0), (0, 0), (0, 0), (0, 0)))
        return jnp.concatenate([prev, tb], axis=2)

    kw, vw = band(k), band(v)
    scores = jnp.einsum('bnqkgd,bnskd->bnkgqs', qb, kw,
                        preferred_element_type=jnp.float32) * (HEAD_DIM ** -0.5)
    q_rel = jnp.arange(BLOCK) + BLOCK
    k_rel = jnp.arange(2 * BLOCK)
    dist = q_rel[:, None] - k_rel[None, :]
    key_abs = (jnp.arange(nb) * BLOCK - BLOCK)[:, None] + k_rel[None, :]
    valid = ((dist >= 0) & (dist < WINDOW))[None] & (key_abs >= 0)[:, None, :]
    slopes = alibi_slopes(N_HEADS).reshape(N_KV_HEADS, GQA_GROUP)
    logits = scores - slopes[:, :, None, None] * dist.astype(jnp.float32)
    logits = jnp.where(valid[None, :, None, None], logits, -jnp.inf)
    sink = jnp.broadcast_to(sinks.astype(jnp.float32).reshape(N_KV_HEADS, GQA_GROUP, 1, 1),
                            logits.shape[:-1] + (1,))
    probs = jax.nn.softmax(jnp.concatenate([logits, sink], axis=-1), axis=-1)[..., :-1]
    out = jnp.einsum('bnkgqs,bnskd->bnqkgd', probs.astype(v.dtype), vw)
    return out.reshape(b, s, D_ATTN)


def short_conv(u, w):
    return lax.conv_general_dilated(u, w[:, None, :], window_strides=(1,),
                                    padding=[(CONV_WIDTH - 1, 0)],
                                    dimension_numbers=('NWC', 'WIO', 'NWC'),
                                    feature_group_count=D_CONV)


def swiglu(h, w_gate_up, w_down):
    g, u = jnp.split(h @ w_gate_up, 2, axis=-1)
    return (jax.nn.silu(g) * u) @ w_down


def moe_swiglu(h, w_router, w_gate_up, w_down):
    t = h.reshape(-1, D_MODEL)
    logits = (t @ w_router).astype(jnp.float32)
    top_v, top_i = lax.top_k(logits, TOP_K)
    top_w = jax.nn.softmax(top_v, axis=-1)
    gates = jnp.sum(jax.nn.one_hot(top_i, N_EXPERTS, dtype=jnp.float32) * top_w[..., None], axis=1)
    out = jnp.zeros_like(t)
    for e in range(N_EXPERTS):
        out = out + gates[:, e:e + 1].astype(t.dtype) * swiglu(t, w_gate_up[e], w_down[e])
    return out.reshape(h.shape)


def setup_inputs(seed: int = 0) -> dict:
    key = jax.random.key(seed)
    ks = jax.random.split(key, 20)

    def nrm(k, shape, scale):
        return jax.random.normal(k, shape, jnp.float32) * scale

    return {
        "x": nrm(ks[0], (BATCH, SEQ, D_MODEL), 1.0),
        "c": nrm(ks[1], (BATCH, D_MODEL), 1.0),
        "ada_w": nrm(ks[2], (DEPTH, D_MODEL, N_MOD * D_MODEL), 0.5 * D_MODEL ** -0.5),
        "ada_b": nrm(ks[3], (DEPTH, N_MOD * D_MODEL), 0.02),
        "norm_mix": 1.0 + nrm(ks[4], (DEPTH, D_MODEL), 0.05),
        "w_in": nrm(ks[5], (DEPTH, D_MODEL, D_IN), D_MODEL ** -0.5),
        "q_norm": 1.0 + nrm(ks[6], (DEPTH, HEAD_DIM), 0.05),
        "k_norm": 1.0 + nrm(ks[7], (DEPTH, HEAD_DIM), 0.05),
        "attn_sinks": nrm(ks[8], (DEPTH, N_HEADS), 0.5),
        "conv_w": nrm(ks[9], (DEPTH, CONV_WIDTH, D_CONV), CONV_WIDTH ** -0.5),
        "w_attn_branch": nrm(ks[10], (DEPTH, D_ATTN, D_MODEL), D_ATTN ** -0.5),
        "w_conv_branch": nrm(ks[11], (DEPTH, D_CONV, D_MODEL), D_CONV ** -0.5),
        "w_out": nrm(ks[12], (DEPTH, D_MODEL, D_MODEL), D_MODEL ** -0.5),
        "norm_ffn": 1.0 + nrm(ks[13], (DEPTH, D_MODEL), 0.05),
        "ffn_w_gate_up": nrm(ks[14], (N_DENSE, D_MODEL, 2 * D_FF), D_MODEL ** -0.5),
        "ffn_w_down": nrm(ks[15], (N_DENSE, D_FF, D_MODEL), D_FF ** -0.5),
        "moe_w_router": nrm(ks[16], (N_MOE, D_MODEL, N_EXPERTS), D_MODEL ** -0.5),
        "moe_w_gate_up": nrm(ks[17], (N_MOE, N_EXPERTS, D_MODEL, 2 * MOE_D_FF), D_MODEL ** -0.5),
        "moe_w_down": nrm(ks[18], (N_MOE, N_EXPERTS, MOE_D_FF, D_MODEL), MOE_D_FF ** -0.5),
    }


def reference(x, c, ada_w, ada_b, norm_mix, w_in, q_norm, k_norm, attn_sinks, conv_w,
              w_attn_branch, w_conv_branch, w_out, norm_ffn, ffn_w_gate_up, ffn_w_down,
              moe_w_router, moe_w_gate_up, moe_w_down):
    b, s = x.shape[0], x.shape[1]
    for l in range(DEPTH):
        mod = jax.nn.silu(c) @ ada_w[l] + ada_b[l]
        shift_m, scale_m, gate_m, shift_f, scale_f, gate_f = jnp.split(mod[:, None, :], N_MOD, axis=-1)

        h = rms_norm(x, norm_mix[l]) * (1.0 + scale_m) + shift_m
        q, k, v, conv_b, conv_c, conv_x, g_attn, g_conv = jnp.split(h @ w_in[l], IN_SPLITS, axis=-1)
        q = rms_norm(q.reshape(b, s, N_HEADS, HEAD_DIM), q_norm[l])
        k = rms_norm(k.reshape(b, s, N_KV_HEADS, HEAD_DIM), k_norm[l])
        v = v.reshape(b, s, N_KV_HEADS, HEAD_DIM)
        attn = sliding_window_attention(q, k, v, attn_sinks[l])
        conv = conv_b * short_conv(conv_c * conv_x, conv_w[l])
        merged = (jax.nn.sigmoid(g_attn) * (attn @ w_attn_branch[l])
                  + jax.nn.sigmoid(g_conv) * (conv @ w_conv_branch[l]))
        x = x + gate_m * (merged @ w_out[l])

        h = rms_norm(x, norm_ffn[l]) * (1.0 + scale_f) + shift_f
        if l % 2 == 0:
            f = swiglu(h, ffn_w_gate_up[l // 2], ffn_w_down[l // 2])
        else:
            f = moe_swiglu(h, moe_w_router[l // 2], moe_w_gate_up[l // 2], moe_w_down[l // 2])
        x = x + gate_f * f
    return x
```

```python
import functools

import jax
import jax.numpy as jnp
from jax import lax
from jax.experimental import pallas as pl
from jax.experimental.pallas import tpu as pltpu

F32 = jnp.float32
BF16 = jnp.bfloat16

HEAD_DIM = 64
GQA_GROUP = 4
ATTN_BLOCK = 128
TOP_K = 2
EPS = 1e-6

LANES = 128
BF16_ROWS = 16
V7X_VMEM_BYTES = 64 * 1024 * 1024
VMEM_LIMIT = V7X_VMEM_BYTES - 8 * 1024 * 1024

ROW_BLOCK = 1024
ROUTE_BLOCK = 512


def _params(*sem):
    return pltpu.CompilerParams(dimension_semantics=sem, vmem_limit_bytes=VMEM_LIMIT)


def _silu(x):
    return x * jax.nn.sigmoid(x)


def _modulated_norm(x, gain, one_plus_scale, shift):
    ms = jnp.mean(x * x, axis=-1, keepdims=True)
    y = (x * lax.rsqrt(ms + EPS)) * gain
    return y * one_plus_scale + shift


def _fill_modulated_norm(x_ref, gain_ref, scale_ref, shift_ref, h_ref, chunk=128):
    gain = gain_ref[...]
    one_plus = 1.0 + scale_ref[...]
    shift = shift_ref[...]

    def body(c, carry):
        r0 = pl.multiple_of(c * chunk, chunk)
        h = _modulated_norm(x_ref[pl.ds(r0, chunk), :], gain, one_plus, shift)
        h_ref[pl.ds(r0, chunk), :] = h.astype(h_ref.dtype)
        return carry

    lax.fori_loop(0, x_ref.shape[0] // chunk, body, 0)


def _adaln_kernel(c_ref, w_ref, b_ref, o_ref):
    s = _silu(c_ref[...]).astype(BF16)
    o_ref[...] = jnp.dot(s, w_ref[...].astype(BF16), preferred_element_type=F32) + b_ref[...]


def _adaln(c, ada_w, ada_b):
    depth, d, n = ada_w.shape
    b = c.shape[0]
    rows = 8 * pl.cdiv(b, 8)
    c_pad = jnp.pad(c, ((0, rows - b), (0, 0)))
    bn = 1024
    out = pl.pallas_call(
        _adaln_kernel,
        out_shape=jax.ShapeDtypeStruct((depth, rows, n), F32),
        grid=(depth, n // bn),
        in_specs=[
            pl.BlockSpec((rows, d), lambda l, j: (0, 0)),
            pl.BlockSpec((None, d, bn), lambda l, j: (l, 0, j)),
            pl.BlockSpec((None, 1, bn), lambda l, j: (l, 0, j)),
        ],
        out_specs=pl.BlockSpec((None, rows, bn), lambda l, j: (l, 0, j)),
        compiler_params=_params("arbitrary", "arbitrary"),
        name="adaln_mod",
    )(c_pad, ada_w, ada_b.reshape(depth, 1, n))
    return out[:, :b]


def _inproj_kernel(x_ref, gain_ref, scale_ref, shift_ref, w_ref, hg_ref, hm_ref, o_ref, h_scr, *,
                   norm_tiles):
    n = pl.program_id(1)

    @pl.when(n == 0)
    def _():
        _fill_modulated_norm(x_ref, gain_ref, scale_ref, shift_ref, h_scr)

    acc = jnp.dot(h_scr[...], w_ref[...].astype(BF16), preferred_element_type=F32)

    @pl.when(n < norm_tiles)
    def _():
        bn = acc.shape[1]
        shift = HEAD_DIM.bit_length() - 1
        r = lax.shift_right_logical(lax.broadcasted_iota(jnp.int32, (bn, bn), 0), shift)
        c = lax.shift_right_logical(lax.broadcasted_iota(jnp.int32, (bn, bn), 1), shift)
        head_mean = jnp.where(r == c, 1.0 / HEAD_DIM, 0.0).astype(BF16)
        ms = jnp.dot((acc * acc).astype(BF16), head_mean, preferred_element_type=F32)
        normed = acc * lax.rsqrt(ms + EPS) * hg_ref[...]
        o_ref[...] = jnp.where(hm_ref[...] > 0.0, normed, acc).astype(o_ref.dtype)

    @pl.when(n >= norm_tiles)
    def _():
        o_ref[...] = acc.astype(o_ref.dtype)


def _inproj(x2, norm_gain, scale, shift, w_in, layer, head_gain, head_mask, seq):
    t, d = x2.shape
    n = w_in.shape[-1]
    bm, bn = ROW_BLOCK, 512
    per_seq = seq // bm
    norm_cols = head_gain.shape[-1]
    kern = functools.partial(_inproj_kernel, norm_tiles=norm_cols // bn)
    return pl.pallas_call(
        kern,
        out_shape=jax.ShapeDtypeStruct((t, n), BF16),
        grid=(t // bm, n // bn),
        in_specs=[
            pl.BlockSpec((bm, d), lambda m, j: (m, 0)),
            pl.BlockSpec((None, 1, d), lambda m, j: (layer, 0, 0)),
            pl.BlockSpec((None, 1, d), lambda m, j: (m // per_seq, 0, 0)),
            pl.BlockSpec((None, 1, d), lambda m, j: (m // per_seq, 0, 0)),
            pl.BlockSpec((None, d, bn), lambda m, j: (layer, 0, j)),
            pl.BlockSpec((1, bn), lambda m, j: (0, jnp.minimum(j, norm_cols // bn - 1))),
            pl.BlockSpec((1, bn), lambda m, j: (0, jnp.minimum(j, norm_cols // bn - 1))),
        ],
        out_specs=pl.BlockSpec((bm, bn), lambda m, j: (m, j)),
        scratch_shapes=[pltpu.VMEM((bm, d), BF16)],
        compiler_params=_params("arbitrary", "arbitrary"),
        name="mixer_in_proj",
    )(x2, norm_gain, scale, shift, w_in, head_gain, head_mask)


def _attn_kernel(sinks_ref, q_ref, kvp_ref, kvc_ref, bias_ref, o_ref, *, n_kv):
    blk = ATTN_BLOCK
    first = pl.program_id(1) == 0
    dkv = n_kv * HEAD_DIM
    q = q_ref[...].astype(F32)
    kvp = kvp_ref[...].astype(F32)
    kvc = kvc_ref[...].astype(F32)
    k_all = jnp.concatenate([kvp[:, :dkv], kvc[:, :dkv]], axis=0)
    v_all = jnp.concatenate([kvp[:, dkv:], kvc[:, dkv:]], axis=0)
    lo = lax.broadcasted_iota(jnp.int32, (1, LANES), 1) < HEAD_DIM
    col = lax.broadcasted_iota(jnp.int32, (1, 2 * blk), 1)
    no_prev = jnp.where(jnp.logical_and(first, col < blk), -jnp.inf, 0.0).astype(F32)

    for j in range(n_kv):
        t, half = divmod(j, 2)
        kp = k_all[:, LANES * t:LANES * (t + 1)]
        vp = v_all[:, LANES * t:LANES * (t + 1)]
        kr = pltpu.roll(kp, HEAD_DIM, 1)
        vr = pltpu.roll(vp, HEAD_DIM, 1)
        k_lo = jnp.where(lo, kr if half else kp, 0.0).astype(BF16)
        v_lo = jnp.where(lo, vr if half else vp, 0.0)
        v_hi = jnp.where(lo, 0.0, vp if half else vr)
        v_stack = jnp.concatenate([v_lo, v_hi], axis=0).astype(BF16)
        base = GQA_GROUP * HEAD_DIM * j
        qa = q[:, base:base + LANES]
        qb = q[:, base + LANES:base + 2 * LANES]
        q_stack = jnp.concatenate(
            [qa, pltpu.roll(qa, HEAD_DIM, 1), qb, pltpu.roll(qb, HEAD_DIM, 1)], axis=0).astype(BF16)
        s = lax.dot_general(q_stack, k_lo, (((1,), (1,)), ((), ())),
                            preferred_element_type=F32)
        probs, inv = [], []
        for g in range(GQA_GROUP):
            h = GQA_GROUP * j + g
            sg = s[blk * g:blk * (g + 1)] + (bias_ref[h] + no_prev)
            sink = sinks_ref[h]
            m = jnp.maximum(jnp.max(sg, axis=-1, keepdims=True), sink)
            p = jnp.exp(sg - m)
            denom = jnp.sum(p, axis=-1, keepdims=True) + jnp.exp(sink - m)
            probs.append(p.astype(BF16))
            inv.append(1.0 / denom)
        for pair in range(GQA_GROUP // 2):
            pp = jnp.concatenate([probs[2 * pair], probs[2 * pair + 1]], axis=1)
            o = jnp.dot(pp, v_stack, preferred_element_type=F32)
            o = o * jnp.where(lo, inv[2 * pair], inv[2 * pair + 1])
            o_ref[:, base + LANES * pair:base + LANES * (pair + 1)] = o.astype(o_ref.dtype)


def _alibi_bias(n_heads):
    blk = ATTN_BLOCK
    slopes = 2.0 ** (-8.0 * jnp.arange(1, n_heads + 1, dtype=F32) / n_heads)
    dist = (jnp.arange(blk) + blk)[:, None] - jnp.arange(2 * blk)[None, :]
    valid = (dist >= 0) & (dist < blk)
    bias = -(slopes[:, None, None] * dist.astype(F32)[None])
    return jnp.where(valid[None], bias, -jnp.inf)


def _attention(y, sinks, bias, batch, seq, d_attn, d_kv):
    t = y.shape[0]
    blk = ATTN_BLOCK
    nb = seq // blk
    kv_w = 2 * d_kv
    kv_col = d_attn // kv_w
    kern = functools.partial(_attn_kernel, n_kv=d_kv // HEAD_DIM)
    grid_spec = pltpu.PrefetchScalarGridSpec(
        num_scalar_prefetch=1,
        grid=(batch, nb),
        in_specs=[
            pl.BlockSpec((blk, d_attn), lambda b, i, s: (b * nb + i, 0)),
            pl.BlockSpec((blk, kv_w), lambda b, i, s: (b * nb + jnp.maximum(i - 1, 0), kv_col)),
            pl.BlockSpec((blk, kv_w), lambda b, i, s: (b * nb + i, kv_col)),
            pl.BlockSpec(bias.shape, lambda b, i, s: (0, 0, 0)),
        ],
        out_specs=pl.BlockSpec((blk, d_attn), lambda b, i, s: (b * nb + i, 0)),
    )
    return pl.pallas_call(
        kern,
        out_shape=jax.ShapeDtypeStruct((t, d_attn), BF16),
        grid_spec=grid_spec,
        compiler_params=_params("arbitrary", "arbitrary"),
        name="swa_attention",
    )(sinks, y, y, y, bias)


HALO = BF16_ROWS


def _branch_kernel(attn_ref, bc_ref, cx_ref, hbc_ref, hcx_ref, cw_ref, wa_ref, wc_ref, ga_ref, gc_ref,
                   o_ref, u_scr, conv_scr, *, blocks_per_seq, taps):
    m = pl.program_id(0)
    n = pl.program_id(1)
    dc = conv_scr.shape[1]
    half = dc // 2
    chunk = 64

    @pl.when(n == 0)
    def _():
        keep = jnp.where(m % blocks_per_seq == 0, 0.0, 1.0).astype(F32)
        hc = jnp.concatenate([hbc_ref[:, dc:], hcx_ref[:, :half]], axis=1).astype(F32)
        hx = hcx_ref[:, half:].astype(F32)
        u_scr[0:HALO, :] = hc * hx * keep

        def fill_u(c, carry):
            r0 = pl.multiple_of(c * chunk, chunk)
            cc = jnp.concatenate([bc_ref[pl.ds(r0, chunk), dc:], cx_ref[pl.ds(r0, chunk), :half]], axis=1)
            xx = cx_ref[pl.ds(r0, chunk), half:]
            u_scr[pl.ds(HALO + r0, chunk), :] = cc.astype(F32) * xx.astype(F32)
            return carry

        lax.fori_loop(0, conv_scr.shape[0] // chunk, fill_u, 0)

        def conv_rows(c, carry):
            r0 = pl.multiple_of(c * chunk, chunk)
            w0 = pl.multiple_of(r0 + HALO - 8, 8)
            win = u_scr[pl.ds(w0, chunk + 8), :]
            acc = cw_ref[taps - 1:taps, :] * win[8:]
            for back in range(1, taps):
                acc = acc + cw_ref[taps - 1 - back:taps - back, :] * pltpu.roll(win, back, 0)[8:]
            b = bc_ref[pl.ds(r0, chunk), :dc].astype(F32)
            conv_scr[pl.ds(r0, chunk), :] = (b * acc).astype(conv_scr.dtype)
            return carry

        lax.fori_loop(0, conv_scr.shape[0] // chunk, conv_rows, 0)

    a = jnp.dot(attn_ref[...], wa_ref[...].astype(BF16), preferred_element_type=F32)
    c = jnp.dot(conv_scr[...], wc_ref[...].astype(BF16), preferred_element_type=F32)
    merged = jax.nn.sigmoid(ga_ref[...].astype(F32)) * a + jax.nn.sigmoid(gc_ref[...].astype(F32)) * c
    o_ref[...] = merged.astype(o_ref.dtype)


def _branch_merge(attn, y, conv_w, w_attn, w_conv, layer, seq, col_b, col_ga, col_gc):
    t, d_attn = attn.shape
    dc = w_conv.shape[1]
    d = w_attn.shape[-1]
    taps = conv_w.shape[1]
    bm, bn = ROW_BLOCK, 512
    wide = dc + dc // 2
    assert col_b % wide == 0 and taps - 1 <= 8
    cb = col_b // wide
    halo_blocks = bm // HALO
    kern = functools.partial(_branch_kernel, blocks_per_seq=seq // bm, taps=taps)

    def halo_map(col):
        return lambda m, j: (jnp.maximum(m * halo_blocks - 1, 0), col)

    return pl.pallas_call(
        kern,
        out_shape=jax.ShapeDtypeStruct((t, d), BF16),
        grid=(t // bm, d // bn),
        in_specs=[
            pl.BlockSpec((bm, d_attn), lambda m, j: (m, 0)),
            pl.BlockSpec((bm, wide), lambda m, j: (m, cb)),
            pl.BlockSpec((bm, wide), lambda m, j: (m, cb + 1)),
            pl.BlockSpec((HALO, wide), halo_map(cb)),
            pl.BlockSpec((HALO, wide), halo_map(cb + 1)),
            pl.BlockSpec((None, taps, dc), lambda m, j: (layer, 0, 0)),
            pl.BlockSpec((None, d_attn, bn), lambda m, j: (layer, 0, j)),
            pl.BlockSpec((None, dc, bn), lambda m, j: (layer, 0, j)),
            pl.BlockSpec((bm, bn), lambda m, j: (m, col_ga // bn + j)),
            pl.BlockSpec((bm, bn), lambda m, j: (m, col_gc // bn + j)),
        ],
        out_specs=pl.BlockSpec((bm, bn), lambda m, j: (m, j)),
        scratch_shapes=[pltpu.VMEM((bm + HALO, dc), F32), pltpu.VMEM((bm, dc), BF16)],
        compiler_params=_params("arbitrary", "arbitrary"),
        name="branch_merge",
    )(attn, y, y, y, y, conv_w, w_attn, w_conv, y, y)


def _resid_proj_kernel(lhs_ref, w_ref, x_ref, gate_ref, o_ref):
    acc = jnp.dot(lhs_ref[...], w_ref[...].astype(BF16), preferred_element_type=F32)
    o_ref[...] = x_ref[...] + gate_ref[...] * acc


def _resid_proj(lhs, w, layer, x2, gate, seq):
    t, k = lhs.shape
    d = w.shape[-1]
    bm, bn = ROW_BLOCK, 512
    per_seq = seq // bm
    return pl.pallas_call(
        _resid_proj_kernel,
        out_shape=jax.ShapeDtypeStruct((t, d), F32),
        grid=(t // bm, d // bn),
        in_specs=[
            pl.BlockSpec((bm, k), lambda m, j: (m, 0)),
            pl.BlockSpec((None, k, bn), lambda m, j: (layer, 0, j)),
            pl.BlockSpec((bm, bn), lambda m, j: (m, j)),
            pl.BlockSpec((None, 1, bn), lambda m, j: (m // per_seq, 0, j)),
        ],
        out_specs=pl.BlockSpec((bm, bn), lambda m, j: (m, j)),
        compiler_params=_params("arbitrary", "arbitrary"),
        name="mixer_out_proj",
    )(lhs, w, x2, gate)


def _ffn_up_kernel(x_ref, gain_ref, scale_ref, shift_ref, wg_ref, wu_ref, o_ref, h_scr):
    @pl.when(pl.program_id(1) == 0)
    def _():
        _fill_modulated_norm(x_ref, gain_ref, scale_ref, shift_ref, h_scr)

    h = h_scr[...]
    g = jnp.dot(h, wg_ref[...].astype(BF16), preferred_element_type=F32)
    u = jnp.dot(h, wu_ref[...].astype(BF16), preferred_element_type=F32)
    o_ref[...] = (_silu(g) * u).astype(o_ref.dtype)


def _ffn_up(x2, norm_gain, scale, shift, w_gate_up, layer, li, seq):
    t, d = x2.shape
    f = w_gate_up.shape[-1] // 2
    bm, bn = ROW_BLOCK, 256
    per_seq = seq // bm
    return pl.pallas_call(
        _ffn_up_kernel,
        out_shape=jax.ShapeDtypeStruct((t, f), BF16),
        grid=(t // bm, f // bn),
        in_specs=[
            pl.BlockSpec((bm, d), lambda m, j: (m, 0)),
            pl.BlockSpec((None, 1, d), lambda m, j: (layer, 0, 0)),
            pl.BlockSpec((None, 1, d), lambda m, j: (m // per_seq, 0, 0)),
            pl.BlockSpec((None, 1, d), lambda m, j: (m // per_seq, 0, 0)),
            pl.BlockSpec((None, d, bn), lambda m, j: (li, 0, j)),
            pl.BlockSpec((None, d, bn), lambda m, j: (li, 0, f // bn + j)),
        ],
        out_specs=pl.BlockSpec((bm, bn), lambda m, j: (m, j)),
        scratch_shapes=[pltpu.VMEM((bm, d), BF16)],
        compiler_params=_params("arbitrary", "arbitrary"),
        name="ffn_gate_up",
    )(x2, norm_gain, scale, shift, w_gate_up, w_gate_up)


def _ffn_down_kernel(a_ref, w_ref, x_ref, gate_ref, o_ref, acc_ref):
    k = pl.program_id(2)
    part = jnp.dot(a_ref[...], w_ref[...].astype(BF16), preferred_element_type=F32)

    @pl.when(k == 0)
    def _():
        acc_ref[...] = part

    @pl.when(k > 0)
    def _():
        acc_ref[...] += part

    @pl.when(k == pl.num_programs(2) - 1)
    def _():
        o_ref[...] = x_ref[...] + gate_ref[...] * acc_ref[...]


def _ffn_down(a, w_down, li, x2, gate, seq):
    t, f = a.shape
    d = w_down.shape[-1]
    bm, bn, bk = ROW_BLOCK, 1024, 512
    per_seq = seq // bm
    return pl.pallas_call(
        _ffn_down_kernel,
        out_shape=jax.ShapeDtypeStruct((t, d), F32),
        grid=(t // bm, d // bn, f // bk),
        in_specs=[
            pl.BlockSpec((bm, bk), lambda m, j, k: (m, k)),
            pl.BlockSpec((None, bk, bn), lambda m, j, k: (li, k, j)),
            pl.BlockSpec((bm, bn), lambda m, j, k: (m, j)),
            pl.BlockSpec((None, 1, bn), lambda m, j, k: (m // per_seq, 0, j)),
        ],
        out_specs=pl.BlockSpec((bm, bn), lambda m, j, k: (m, j)),
        scratch_shapes=[pltpu.VMEM((bm, bn), F32)],
        compiler_params=_params("arbitrary", "arbitrary", "arbitrary"),
        name="ffn_down",
    )(a, w_down, x2, gate)


def _router_kernel(x_ref, gain_ref, scale_ref, shift_ref, r_ref, h_ref, meta_ref, cnt_ref,
                   tri_scr, base_scr, logit_scr, *, n_experts):
    i = pl.program_id(0)
    rows = x_ref.shape[0]
    chunk = 64

    @pl.when(i == 0)
    def _():
        r = lax.broadcasted_iota(jnp.int32, (rows, rows), 0)
        c = lax.broadcasted_iota(jnp.int32, (rows, rows), 1)
        tri_scr[...] = jnp.where(c < r, 1.0, 0.0).astype(BF16)
        base_scr[...] = jnp.zeros(base_scr.shape, F32)

    gain = gain_ref[...]
    one_plus = 1.0 + scale_ref[...]
    shift = shift_ref[...]
    lane_c = lax.broadcasted_iota(jnp.int32, (chunk, LANES), 1)

    def body(c, carry):
        r0 = pl.multiple_of(c * chunk, chunk)
        h = _modulated_norm(x_ref[pl.ds(r0, chunk), :], gain, one_plus, shift)
        h_ref[pl.ds(r0, chunk), :] = h
        lg = jnp.full((chunk, LANES), -jnp.inf, F32)
        for e in range(n_experts):
            col = jnp.sum(h * r_ref[e:e + 1, :], axis=-1, keepdims=True)
            lg = jnp.where(lane_c == e, col, lg)
        logit_scr[pl.ds(r0, chunk), :] = lg
        return carry

    lax.fori_loop(0, rows // chunk, body, 0)

    lg = logit_scr[...]
    lane = lax.broadcasted_iota(jnp.int32, lg.shape, 1).astype(F32)
    m1 = jnp.max(lg, axis=-1, keepdims=True)
    i1 = jnp.min(jnp.where(lg == m1, lane, float(LANES)), axis=-1, keepdims=True)
    lg2 = jnp.where(lane == i1, -jnp.inf, lg)
    m2 = jnp.max(lg2, axis=-1, keepdims=True)
    i2 = jnp.min(jnp.where(lg2 == m2, lane, float(LANES)), axis=-1, keepdims=True)
    ex = jnp.exp(m2 - m1)
    w1 = 1.0 / (1.0 + ex)
    w2 = ex / (1.0 + ex)
    onehot = jnp.where(jnp.logical_or(lane == i1, lane == i2), 1.0, 0.0)
    ranks = jnp.dot(tri_scr[...], onehot.astype(BF16), preferred_element_type=F32) + base_scr[...]
    r1 = jnp.sum(jnp.where(lane == i1, ranks, 0.0), axis=-1, keepdims=True)
    r2 = jnp.sum(jnp.where(lane == i2, ranks, 0.0), axis=-1, keepdims=True)
    meta = jnp.zeros(lg.shape, F32)
    for pos, val in enumerate((i1, i2, r1, r2, w1, w2)):
        meta = jnp.where(lane == float(pos), val, meta)
    meta_ref[...] = meta
    new_base = base_scr[...] + jnp.sum(onehot, axis=0, keepdims=True)
    base_scr[...] = new_base
    cnt_ref[...] = new_base


def _router(x2, norm_gain, scale, shift, w_router_t, layer, seq):
    t, d = x2.shape
    e = w_router_t.shape[0]
    bm = ROUTE_BLOCK
    per_seq = seq // bm
    kern = functools.partial(_router_kernel, n_experts=e)
    return pl.pallas_call(
        kern,
        out_shape=(jax.ShapeDtypeStruct((t, d), F32),
                   jax.ShapeDtypeStruct((t, LANES), F32),
                   jax.ShapeDtypeStruct((1, LANES), F32)),
        grid=(t // bm,),
        in_specs=[
            pl.BlockSpec((bm, d), lambda m: (m, 0)),
            pl.BlockSpec((None, 1, d), lambda m: (layer, 0, 0)),
            pl.BlockSpec((None, 1, d), lambda m: (m // per_seq, 0, 0)),
            pl.BlockSpec((None, 1, d), lambda m: (m // per_seq, 0, 0)),
            pl.BlockSpec((e, d), lambda m: (0, 0)),
        ],
        out_specs=(pl.BlockSpec((bm, d), lambda m: (m, 0)),
                   pl.BlockSpec((bm, LANES), lambda m: (m, 0)),
                   pl.BlockSpec((1, LANES), lambda m: (0, 0))),
        scratch_shapes=[pltpu.VMEM((bm, bm), BF16), pltpu.VMEM((1, LANES), F32),
                        pltpu.VMEM((bm, LANES), F32)],
        compiler_params=_params("arbitrary"),
        name="moe_router",
    )(x2, norm_gain, scale, shift, w_router_t)


def _dispatch_kernel(s1_ref, s2_ref, h_ref, init_ref, xs_ref, sem):
    del init_ref
    rows = h_ref.shape[0]

    def row_copy(i, slot_ref):
        return pltpu.make_async_copy(h_ref.at[pl.ds(i, 1)], xs_ref.at[pl.ds(slot_ref[0, i], 1)], sem)

    def body(i, carry):
        row_copy(i, s1_ref).start()
        row_copy(i, s2_ref).start()
        return carry

    lax.fori_loop(0, rows, body, 0, unroll=8)
    for _ in range(TOP_K):
        pltpu.make_async_copy(h_ref, xs_ref.at[pl.ds(0, rows)], sem).wait()


def _dispatch(h, slot1, slot2, n_slots):
    t, d = h.shape
    bm = ROW_BLOCK
    nblk = t // bm
    slots = [s.reshape(nblk, 1, bm) for s in (slot1, slot2)]
    smem_spec = pl.BlockSpec((None, 1, bm), lambda m: (m, 0, 0), memory_space=pltpu.SMEM)
    return pl.pallas_call(
        _dispatch_kernel,
        out_shape=jax.ShapeDtypeStruct((n_slots, d), F32),
        grid=(nblk,),
        in_specs=[smem_spec, smem_spec,
                  pl.BlockSpec((bm, d), lambda m: (m, 0)),
                  pl.BlockSpec(memory_space=pl.ANY)],
        out_specs=pl.BlockSpec(memory_space=pl.ANY),
        scratch_shapes=[pltpu.SemaphoreType.DMA(())],
        input_output_aliases={3: 0},
        compiler_params=_params("arbitrary"),
        name="moe_dispatch",
    )(slots[0], slots[1], h, jnp.zeros((n_slots, d), F32))


def _moe_up_kernel(be_ref, nl_ref, xs_ref, wg_ref, wu_ref, o_ref, h_scr):
    del be_ref
    live = pl.program_id(0) < nl_ref[0]
    chunk = 128

    @pl.when(jnp.logical_and(live, pl.program_id(1) == 0))
    def _():
        def body(c, carry):
            r0 = pl.multiple_of(c * chunk, chunk)
            h_scr[pl.ds(r0, chunk), :] = xs_ref[pl.ds(r0, chunk), :].astype(h_scr.dtype)
            return carry

        lax.fori_loop(0, h_scr.shape[0] // chunk, body, 0)

    @pl.when(live)
    def _():
        h = h_scr[...]
        g = jnp.dot(h, wg_ref[...].astype(BF16), preferred_element_type=F32)
        u = jnp.dot(h, wu_ref[...].astype(BF16), preferred_element_type=F32)
        o_ref[...] = (_silu(g) * u).astype(o_ref.dtype)

    @pl.when(jnp.logical_not(live))
    def _():
        o_ref[...] = jnp.zeros(o_ref.shape, o_ref.dtype)


def _moe_up(xs, w_gate_up, li, blk_expert, n_live):
    s, d = xs.shape
    f = w_gate_up.shape[-1] // 2
    bm, bn = ROW_BLOCK, 256
    nt = f // bn

    def row(b, j, be, nl):
        return jnp.minimum(b, nl[0] - 1)

    def col(b, j, be, nl):
        return jnp.where(b < nl[0], j, nt - 1)

    grid_spec = pltpu.PrefetchScalarGridSpec(
        num_scalar_prefetch=2,
        grid=(s // bm, nt),
        in_specs=[
            pl.BlockSpec((bm, d), lambda b, j, be, nl: (row(b, j, be, nl), 0)),
            pl.BlockSpec((None, None, d, bn),
                         lambda b, j, be, nl: (li, be[row(b, j, be, nl)], 0, col(b, j, be, nl))),
            pl.BlockSpec((None, None, d, bn),
                         lambda b, j, be, nl: (li, be[row(b, j, be, nl)], 0, nt + col(b, j, be, nl))),
        ],
        out_specs=pl.BlockSpec((bm, bn), lambda b, j, be, nl: (b, j)),
        scratch_shapes=[pltpu.VMEM((bm, d), BF16)],
    )
    return pl.pallas_call(
        _moe_up_kernel,
        out_shape=jax.ShapeDtypeStruct((s, f), BF16),
        grid_spec=grid_spec,
        compiler_params=_params("arbitrary", "arbitrary"),
        name="moe_gate_up",
    )(blk_expert, n_live, xs, w_gate_up, w_gate_up)


def _moe_down_kernel(be_ref, nl_ref, a_ref, w_ref, o_ref, acc_ref):
    del be_ref
    live = pl.program_id(0) < nl_ref[0]
    k = pl.program_id(2)
    last = k == pl.num_programs(2) - 1

    @pl.when(live)
    def _():
        part = jnp.dot(a_ref[...], w_ref[...].astype(BF16), preferred_element_type=F32)

        @pl.when(k == 0)
        def _():
            acc_ref[...] = part

        @pl.when(k > 0)
        def _():
            acc_ref[...] += part

        @pl.when(last)
        def _():
            o_ref[...] = acc_ref[...]

    @pl.when(jnp.logical_and(jnp.logical_not(live), last))
    def _():
        o_ref[...] = jnp.zeros(o_ref.shape, o_ref.dtype)


def _moe_down(a, w_down, li, blk_expert, n_live):
    s, f = a.shape
    d = w_down.shape[-1]
    bm, bn, bk = ROW_BLOCK, 1024, 512
    nk = f // bk

    def row(b, nl):
        return jnp.minimum(b, nl[0] - 1)

    def kk(b, k, nl):
        return jnp.where(b < nl[0], k, nk - 1)

    grid_spec = pltpu.PrefetchScalarGridSpec(
        num_scalar_prefetch=2,
        grid=(s // bm, d // bn, nk),
        in_specs=[
            pl.BlockSpec((bm, bk), lambda b, j, k, be, nl: (row(b, nl), kk(b, k, nl))),
            pl.BlockSpec((None, None, bk, bn),
                         lambda b, j, k, be, nl: (li, be[row(b, nl)], kk(b, k, nl), j)),
        ],
        out_specs=pl.BlockSpec((bm, bn), lambda b, j, k, be, nl: (b, j)),
        scratch_shapes=[pltpu.VMEM((bm, bn), F32)],
    )
    return pl.pallas_call(
        _moe_down_kernel,
        out_shape=jax.ShapeDtypeStruct((s, d), F32),
        grid_spec=grid_spec,
        compiler_params=_params("arbitrary", "arbitrary", "arbitrary"),
        name="moe_down",
    )(blk_expert, n_live, a, w_down)


def _combine_kernel(s1_ref, s2_ref, x_ref, gate_ref, meta_ref, ys_ref, o_ref, g1, g2, sem):
    rows = x_ref.shape[0]
    chunk = 128

    def row_copy(i, slot_ref, dst, which):
        return pltpu.make_async_copy(ys_ref.at[pl.ds(slot_ref[0, i], 1)], dst.at[pl.ds(i, 1)],
                                     sem.at[which])

    def body(i, carry):
        row_copy(i, s1_ref, g1, 0).start()
        row_copy(i, s2_ref, g2, 1).start()
        return carry

    lax.fori_loop(0, rows, body, 0, unroll=8)
    pltpu.make_async_copy(ys_ref.at[pl.ds(0, rows)], g1, sem.at[0]).wait()
    pltpu.make_async_copy(ys_ref.at[pl.ds(0, rows)], g2, sem.at[1]).wait()

    gate = gate_ref[...]

    def mix(c, carry):
        r0 = pl.multiple_of(c * chunk, chunk)
        sl = pl.ds(r0, chunk)
        w1 = meta_ref[sl, 4:5]
        w2 = meta_ref[sl, 5:6]
        o_ref[sl, :] = x_ref[sl, :] + gate * (w1 * g1[sl, :] + w2 * g2[sl, :])
        return carry

    lax.fori_loop(0, rows // chunk, mix, 0)


def _combine(ys, slot1, slot2, meta, x2, gate, seq):
    t, d = x2.shape
    bm = ROUTE_BLOCK
    nblk = t // bm
    per_seq = seq // bm
    slots = [s.reshape(nblk, 1, bm) for s in (slot1, slot2)]
    smem_spec = pl.BlockSpec((None, 1, bm), lambda m: (m, 0, 0), memory_space=pltpu.SMEM)
    return pl.pallas_call(
        _combine_kernel,
        out_shape=jax.ShapeDtypeStruct((t, d), F32),
        grid=(nblk,),
        in_specs=[smem_spec, smem_spec,
                  pl.BlockSpec((bm, d), lambda m: (m, 0)),
                  pl.BlockSpec((None, 1, d), lambda m: (m // per_seq, 0, 0)),
                  pl.BlockSpec((bm, LANES), lambda m: (m, 0)),
                  pl.BlockSpec(memory_space=pl.ANY)],
        out_specs=pl.BlockSpec((bm, d), lambda m: (m, 0)),
        scratch_shapes=[pltpu.VMEM((bm, d), F32), pltpu.VMEM((bm, d), F32),
                        pltpu.SemaphoreType.DMA((2,))],
        compiler_params=_params("arbitrary"),
        name="moe_combine",
    )(slots[0], slots[1], x2, gate, meta, ys)


def _moe(x2, norm_gain, scale, shift, gate, w_router, w_gate_up, w_down, layer, li, seq):
    t, d = x2.shape
    e = w_router.shape[-1]
    blk = ROW_BLOCK
    h, meta, counts = _router(x2, norm_gain, scale, shift, jnp.transpose(w_router[li]), layer, seq)

    cnt = counts[0, :e].astype(jnp.int32)
    nblk = (cnt + blk - 1) // blk
    blk_end = jnp.cumsum(nblk)
    off = (blk_end - nblk) * blk
    n_blocks = TOP_K * t // blk + e
    n_live = blk_end[-1:].astype(jnp.int32)
    blk_expert = jnp.minimum(
        jnp.searchsorted(blk_end, jnp.arange(n_blocks, dtype=jnp.int32), side="right"), e - 1
    ).astype(jnp.int32)
    ids = meta[:, 0:2].astype(jnp.int32)
    ranks = meta[:, 2:4].astype(jnp.int32)
    slots = off[ids] + ranks
    slot1, slot2 = slots[:, 0], slots[:, 1]

    xs = _dispatch(h, slot1, slot2, n_blocks * blk)
    a = _moe_up(xs, w_gate_up, li, blk_expert, n_live)
    ys = _moe_down(a, w_down, li, blk_expert, n_live)
    return _combine(ys, slot1, slot2, meta, x2, gate, seq)


def kernel(x, c, ada_w, ada_b, norm_mix, w_in, q_norm, k_norm, attn_sinks, conv_w, w_attn_branch,
           w_conv_branch, w_out, norm_ffn, ffn_w_gate_up, ffn_w_down, moe_w_router, moe_w_gate_up,
           moe_w_down):
    batch, seq, d = x.shape
    depth = ada_w.shape[0]
    d_attn = w_attn_branch.shape[1]
    d_conv = w_conv_branch.shape[1]
    n_heads = d_attn // HEAD_DIM
    d_kv = (n_heads // GQA_GROUP) * HEAD_DIM
    col_b = d_attn + 2 * d_kv
    col_ga = col_b + 3 * d_conv
    col_gc = col_ga + d
    assert w_in.shape[-1] == col_gc + d and seq % ROW_BLOCK == 0
    assert 2 * HEAD_DIM == LANES and GQA_GROUP == 4 and (n_heads // GQA_GROUP) % 2 == 0

    n_mod = ada_w.shape[-1] // d
    mod = _adaln(c, ada_w, ada_b).reshape(depth, batch, n_mod, 1, d)
    bias = _alibi_bias(n_heads)
    x2 = x.reshape(batch * seq, d)

    for l in range(depth):
        shift_m, scale_m, gate_m, shift_f, scale_f, gate_f = (mod[l, :, i] for i in range(n_mod))

        reps_q, reps_k = d_attn // HEAD_DIM, d_kv // HEAD_DIM
        head_gain = jnp.concatenate([jnp.tile(q_norm[l] * HEAD_DIM ** -0.5, reps_q),
                                     jnp.tile(k_norm[l], reps_k), jnp.ones((d_kv,), F32)])[None]
        head_mask = jnp.concatenate([jnp.ones((d_attn + d_kv,), F32), jnp.zeros((d_kv,), F32)])[None]

        y = _inproj(x2, norm_mix.reshape(depth, 1, d), scale_m, shift_m, w_in, l, head_gain, head_mask, seq)
        attn = _attention(y, attn_sinks[l], bias, batch, seq, d_attn, d_kv)
        merged = _branch_merge(attn, y, conv_w, w_attn_branch, w_conv_branch, l, seq, col_b, col_ga, col_gc)
        x2 = _resid_proj(merged, w_out, l, x2, gate_m, seq)

        norm_f = norm_ffn.reshape(depth, 1, d)
        if l % 2 == 0:
            a = _ffn_up(x2, norm_f, scale_f, shift_f, ffn_w_gate_up, l, l // 2, seq)
            x2 = _ffn_down(a, ffn_w_down, l // 2, x2, gate_f, seq)
        else:
            x2 = _moe(x2, norm_f, scale_f, shift_f, gate_f, moe_w_router, moe_w_gate_up, moe_w_down,
                      l, l // 2, seq)
    return x2.reshape(batch, seq, d)
```

```python
import functools

import jax
import jax.numpy as jnp
from jax import lax
from jax.experimental import pallas as pl
from jax.experimental.pallas import tpu as pltpu

F32 = jnp.float32
BF16 = jnp.bfloat16

HEAD_DIM = 64
GQA_GROUP = 4
ATTN_BLOCK = 128
TOP_K = 2
EPS = 1e-6

LANES = 128
BF16_ROWS = 16
V7X_VMEM_BYTES = 64 * 1024 * 1024
VMEM_LIMIT = V7X_VMEM_BYTES - 8 * 1024 * 1024

ROW_BLOCK = 1024
ROUTE_BLOCK = 512


def _params(*sem):
    return pltpu.CompilerParams(dimension_semantics=sem, vmem_limit_bytes=VMEM_LIMIT)


def _silu(x):
    return x * jax.nn.sigmoid(x)


def _modulated_norm(x, gain, one_plus_scale, shift):
    ms = jnp.mean(x * x, axis=-1, keepdims=True)
    y = (x * lax.rsqrt(ms + EPS)) * gain
    return y * one_plus_scale + shift


def _fill_modulated_norm(x_ref, gain_ref, scale_ref, shift_ref, h_ref, chunk=128):
    gain = gain_ref[...]
    one_plus = 1.0 + scale_ref[...]
    shift = shift_ref[...]

    def body(c, carry):
        r0 = pl.multiple_of(c * chunk, chunk)
        h = _modulated_norm(x_ref[pl.ds(r0, chunk), :], gain, one_plus, shift)
        h_ref[pl.ds(r0, chunk), :] = h.astype(h_ref.dtype)
        return carry

    lax.fori_loop(0, x_ref.shape[0] // chunk, body, 0)


def _adaln_kernel(c_ref, w_ref, b_ref, o_ref):
    s = _silu(c_ref[...]).astype(BF16)
    o_ref[...] = jnp.dot(s, w_ref[...].astype(BF16), preferred_element_type=F32) + b_ref[...]


def _adaln(c, ada_w, ada_b):
    depth, d, n = ada_w.shape
    b = c.shape[0]
    rows = 8 * pl.cdiv(b, 8)
    c_pad = jnp.pad(c, ((0, rows - b), (0, 0)))
    bn = 1024
    out = pl.pallas_call(
        _adaln_kernel,
        out_shape=jax.ShapeDtypeStruct((depth, rows, n), F32),
        grid=(depth, n // bn),
        in_specs=[
            pl.BlockSpec((rows, d), lambda l, j: (0, 0)),
            pl.BlockSpec((None, d, bn), lambda l, j: (l, 0, j)),
            pl.BlockSpec((None, 1, bn), lambda l, j: (l, 0, j)),
        ],
        out_specs=pl.BlockSpec((None, rows, bn), lambda l, j: (l, 0, j)),
        compiler_params=_params("arbitrary", "arbitrary"),
        name="adaln_mod",
    )(c_pad, ada_w, ada_b.reshape(depth, 1, n))
    return out[:, :b]


def _inproj_kernel(x_ref, gain_ref, scale_ref, shift_ref, w_ref, hg_ref, hm_ref, o_ref, h_scr, *,
                   norm_tiles):
    n = pl.program_id(1)

    @pl.when(n == 0)
    def _():
        _fill_modulated_norm(x_ref, gain_ref, scale_ref, shift_ref, h_scr)

    acc = jnp.dot(h_scr[...], w_ref[...].astype(BF16), preferred_element_type=F32)

    @pl.when(n < norm_tiles)
    def _():
        bn = acc.shape[1]
        shift = HEAD_DIM.bit_length() - 1
        r = lax.shift_right_logical(lax.broadcasted_iota(jnp.int32, (bn, bn), 0), shift)
        c = lax.shift_right_logical(lax.broadcasted_iota(jnp.int32, (bn, bn), 1), shift)
        head_mean = jnp.where(r == c, 1.0 / HEAD_DIM, 0.0).astype(BF16)
        ms = jnp.dot((acc * acc).astype(BF16), head_mean, preferred_element_type=F32)
        normed = acc * lax.rsqrt(ms + EPS) * hg_ref[...]
        o_ref[...] = jnp.where(hm_ref[...] > 0.0, normed, acc).astype(o_ref.dtype)

    @pl.when(n >= norm_tiles)
    def _():
        o_ref[...] = acc.astype(o_ref.dtype)


def _inproj(x2, norm_gain, scale, shift, w_in, layer, head_gain, head_mask, seq):
    t, d = x2.shape
    n = w_in.shape[-1]
    bm, bn = ROW_BLOCK, 512
    per_seq = seq // bm
    norm_cols = head_gain.shape[-1]
    kern = functools.partial(_inproj_kernel, norm_tiles=norm_cols // bn)
    return pl.pallas_call(
        kern,
        out_shape=jax.ShapeDtypeStruct((t, n), BF16),
        grid=(t // bm, n // bn),
        in_specs=[
            pl.BlockSpec((bm, d), lambda m, j: (m, 0)),
            pl.BlockSpec((None, 1, d), lambda m, j: (layer, 0, 0)),
            pl.BlockSpec((None, 1, d), lambda m, j: (m // per_seq, 0, 0)),
            pl.BlockSpec((None, 1, d), lambda m, j: (m // per_seq, 0, 0)),
            pl.BlockSpec((None, d, bn), lambda m, j: (layer, 0, j)),
            pl.BlockSpec((1, bn), lambda m, j: (0, jnp.minimum(j, norm_cols // bn - 1))),
            pl.BlockSpec((1, bn), lambda m, j: (0, jnp.minimum(j, norm_cols // bn - 1))),
        ],
        out_specs=pl.BlockSpec((bm, bn), lambda m, j: (m, j)),
        scratch_shapes=[pltpu.VMEM((bm, d), BF16)],
        compiler_params=_params("arbitrary", "arbitrary"),
        name="mixer_in_proj",
    )(x2, norm_gain, scale, shift, w_in, head_gain, head_mask)


def _attn_kernel(sinks_ref, q_ref, kvp_ref, kvc_ref, bias_ref, o_ref, *, n_kv):
    blk = ATTN_BLOCK
    first = pl.program_id(1) == 0
    dkv = n_kv * HEAD_DIM
    q = q_ref[...].astype(F32)
    kvp = kvp_ref[...].astype(F32)
    kvc = kvc_ref[...].astype(F32)
    k_all = jnp.concatenate([kvp[:, :dkv], kvc[:, :dkv]], axis=0)
    v_all = jnp.concatenate([kvp[:, dkv:], kvc[:, dkv:]], axis=0)
    lo = lax.broadcasted_iota(jnp.int32, (1, LANES), 1) < HEAD_DIM
    col = lax.broadcasted_iota(jnp.int32, (1, 2 * blk), 1)
    no_prev = jnp.where(jnp.logical_and(first, col < blk), -jnp.inf, 0.0).astype(F32)

    for j in range(n_kv):
        t, half = divmod(j, 2)
        kp = k_all[:, LANES * t:LANES * (t + 1)]
        vp = v_all[:, LANES * t:LANES * (t + 1)]
        kr = pltpu.roll(kp, HEAD_DIM, 1)
        vr = pltpu.roll(vp, HEAD_DIM, 1)
        k_lo = jnp.where(lo, kr if half else kp, 0.0).astype(BF16)
        v_lo = jnp.where(lo, vr if half else vp, 0.0)
        v_hi = jnp.where(lo, 0.0, vp if half else vr)
        v_stack = jnp.concatenate([v_lo, v_hi], axis=0).astype(BF16)
        base = GQA_GROUP * HEAD_DIM * j
        qa = q[:, base:base + LANES]
        qb = q[:, base + LANES:base + 2 * LANES]
        q_stack = jnp.concatenate(
            [qa, pltpu.roll(qa, HEAD_DIM, 1), qb, pltpu.roll(qb, HEAD_DIM, 1)], axis=0).astype(BF16)
        s = lax.dot_general(q_stack, k_lo, (((1,), (1,)), ((), ())),
                            preferred_element_type=F32)
        probs, inv = [], []
        for g in range(GQA_GROUP):
            h = GQA_GROUP * j + g
            sg = s[blk * g:blk * (g + 1)] + (bias_ref[h] + no_prev)
            sink = sinks_ref[h]
            m = jnp.maximum(jnp.max(sg, axis=-1, keepdims=True), sink)
            p = jnp.exp(sg - m)
            denom = jnp.sum(p, axis=-1, keepdims=True) + jnp.exp(sink - m)
            probs.append(p.astype(BF16))
            inv.append(1.0 / denom)
        for pair in range(GQA_GROUP // 2):
            pp = jnp.concatenate([probs[2 * pair], probs[2 * pair + 1]], axis=1)
            o = jnp.dot(pp, v_stack, preferred_element_type=F32)
            o = o * jnp.where(lo, inv[2 * pair], inv[2 * pair + 1])
            o_ref[:, base + LANES * pair:base + LANES * (pair + 1)] = o.astype(o_ref.dtype)


def _alibi_bias(n_heads):
    blk = ATTN_BLOCK
    slopes = 2.0 ** (-8.0 * jnp.arange(1, n_heads + 1, dtype=F32) / n_heads)
    dist = (jnp.arange(blk) + blk)[:, None] - jnp.arange(2 * blk)[None, :]
    valid = (dist >= 0) & (dist < blk)
    bias = -(slopes[:, None, None] * dist.astype(F32)[None])
    return jnp.where(valid[None], bias, -jnp.inf)


def _attention(y, sinks, bias, batch, seq, d_attn, d_kv):
    t = y.shape[0]
    blk = ATTN_BLOCK
    nb = seq // blk
    kv_w = 2 * d_kv
    kv_col = d_attn // kv_w
    kern = functools.partial(_attn_kernel, n_kv=d_kv // HEAD_DIM)
    grid_spec = pltpu.PrefetchScalarGridSpec(
        num_scalar_prefetch=1,
        grid=(batch, nb),
        in_specs=[
            pl.BlockSpec((blk, d_attn), lambda b, i, s: (b * nb + i, 0)),
            pl.BlockSpec((blk, kv_w), lambda b, i, s: (b * nb + jnp.maximum(i - 1, 0), kv_col)),
            pl.BlockSpec((blk, kv_w), lambda b, i, s: (b * nb + i, kv_col)),
            pl.BlockSpec(bias.shape, lambda b, i, s: (0, 0, 0)),
        ],
        out_specs=pl.BlockSpec((blk, d_attn), lambda b, i, s: (b * nb + i, 0)),
    )
    return pl.pallas_call(
        kern,
        out_shape=jax.ShapeDtypeStruct((t, d_attn), BF16),
        grid_spec=grid_spec,
        compiler_params=_params("arbitrary", "arbitrary"),
        name="swa_attention",
    )(sinks, y, y, y, bias)


HALO = BF16_ROWS


def _branch_kernel(attn_ref, bc_ref, cx_ref, hbc_ref, hcx_ref, cw_ref, wa_ref, wc_ref, ga_ref, gc_ref,
                   o_ref, u_scr, conv_scr, *, blocks_per_seq, taps):
    m = pl.program_id(0)
    n = pl.program_id(1)
    dc = conv_scr.shape[1]
    half = dc // 2
    chunk = 64

    @pl.when(n == 0)
    def _():
        keep = jnp.where(m % blocks_per_seq == 0, 0.0, 1.0).astype(F32)
        hc = jnp.concatenate([hbc_ref[:, dc:], hcx_ref[:, :half]], axis=1).astype(F32)
        hx = hcx_ref[:, half:].astype(F32)
        u_scr[0:HALO, :] = hc * hx * keep

        def fill_u(c, carry):
            r0 = pl.multiple_of(c * chunk, chunk)
            cc = jnp.concatenate([bc_ref[pl.ds(r0, chunk), dc:], cx_ref[pl.ds(r0, chunk), :half]], axis=1)
            xx = cx_ref[pl.ds(r0, chunk), half:]
            u_scr[pl.ds(HALO + r0, chunk), :] = cc.astype(F32) * xx.astype(F32)
            return carry

        lax.fori_loop(0, conv_scr.shape[0] // chunk, fill_u, 0)

        def conv_rows(c, carry):
            r0 = pl.multiple_of(c * chunk, chunk)
            w0 = pl.multiple_of(r0 + HALO - 8, 8)
            win = u_scr[pl.ds(w0, chunk + 8), :]
            acc = cw_ref[taps - 1:taps, :] * win[8:]
            for back in range(1, taps):
                acc = acc + cw_ref[taps - 1 - back:taps - back, :] * pltpu.roll(win, back, 0)[8:]
            b = bc_ref[pl.ds(r0, chunk), :dc].astype(F32)
            conv_scr[pl.ds(r0, chunk), :] = (b * acc).astype(conv_scr.dtype)
            return carry

        lax.fori_loop(0, conv_scr.shape[0] // chunk, conv_rows, 0)

    a = jnp.dot(attn_ref[...], wa_ref[...].astype(BF16), preferred_element_type=F32)
    c = jnp.dot(conv_scr[...], wc_ref[...].astype(BF16), preferred_element_type=F32)
    merged = jax.nn.sigmoid(ga_ref[...].astype(F32)) * a + jax.nn.sigmoid(gc_ref[...].astype(F32)) * c
    o_ref[...] = merged.astype(o_ref.dtype)


def _branch_merge(attn, y, conv_w, w_attn, w_conv, layer, seq, col_b, col_ga, col_gc):
    t, d_attn = attn.shape
    dc = w_conv.shape[1]
    d = w_attn.shape[-1]
    taps = conv_w.shape[1]
    bm, bn = ROW_BLOCK, 512
    wide = dc + dc // 2
    assert col_b % wide == 0 and taps - 1 <= 8
    cb = col_b // wide
    halo_blocks = bm // HALO
    kern = functools.partial(_branch_kernel, blocks_per_seq=seq // bm, taps=taps)

    def halo_map(col):
        return lambda m, j: (jnp.maximum(m * halo_blocks - 1, 0), col)

    return pl.pallas_call(
        kern,
        out_shape=jax.ShapeDtypeStruct((t, d), BF16),
        grid=(t // bm, d // bn),
        in_specs=[
            pl.BlockSpec((bm, d_attn), lambda m, j: (m, 0)),
            pl.BlockSpec((bm, wide), lambda m, j: (m, cb)),
            pl.BlockSpec((bm, wide), lambda m, j: (m, cb + 1)),
            pl.BlockSpec((HALO, wide), halo_map(cb)),
            pl.BlockSpec((HALO, wide), halo_map(cb + 1)),
            pl.BlockSpec((None, taps, dc), lambda m, j: (layer, 0, 0)),
            pl.BlockSpec((None, d_attn, bn), lambda m, j: (layer, 0, j)),
            pl.BlockSpec((None, dc, bn), lambda m, j: (layer, 0, j)),
            pl.BlockSpec((bm, bn), lambda m, j: (m, col_ga // bn + j)),
            pl.BlockSpec((bm, bn), lambda m, j: (m, col_gc // bn + j)),
        ],
        out_specs=pl.BlockSpec((bm, bn), lambda m, j: (m, j)),
        scratch_shapes=[pltpu.VMEM((bm + HALO, dc), F32), pltpu.VMEM((bm, dc), BF16)],
        compiler_params=_params("arbitrary", "arbitrary"),
        name="branch_merge",
    )(attn, y, y, y, y, conv_w, w_attn, w_conv, y, y)


def _resid_proj_kernel(lhs_ref, w_ref, x_ref, gate_ref, o_ref):
    acc = jnp.dot(lhs_ref[...], w_ref[...].astype(BF16), preferred_element_type=F32)
    o_ref[...] = x_ref[...] + gate_ref[...] * acc


def _resid_proj(lhs, w, layer, x2, gate, seq, bn, name):
    t, k = lhs.shape
    d = w.shape[-1]
    bm = ROW_BLOCK
    per_seq = seq // bm
    return pl.pallas_call(
        _resid_proj_kernel,
        out_shape=jax.ShapeDtypeStruct((t, d), F32),
        grid=(t // bm, d // bn),
        in_specs=[
            pl.BlockSpec((bm, k), lambda m, j: (m, 0)),
            pl.BlockSpec((None, k, bn), lambda m, j: (layer, 0, j)),
            pl.BlockSpec((bm, bn), lambda m, j: (m, j)),
            pl.BlockSpec((None, 1, bn), lambda m, j: (m // per_seq, 0, j)),
        ],
        out_specs=pl.BlockSpec((bm, bn), lambda m, j: (m, j)),
        compiler_params=_params("arbitrary", "arbitrary"),
        name=name,
    )(lhs, w, x2, gate)


def _ffn_up_kernel(x_ref, gain_ref, scale_ref, shift_ref, wg_ref, wu_ref, o_ref, h_scr):
    @pl.when(pl.program_id(1) == 0)
    def _():
        _fill_modulated_norm(x_ref, gain_ref, scale_ref, shift_ref, h_scr)

    h = h_scr[...]
    g = jnp.dot(h, wg_ref[...].astype(BF16), preferred_element_type=F32)
    u = jnp.dot(h, wu_ref[...].astype(BF16), preferred_element_type=F32)
    o_ref[...] = (_silu(g) * u).astype(o_ref.dtype)


def _ffn_up(x2, norm_gain, scale, shift, w_gate_up, layer, li, seq):
    t, d = x2.shape
    f = w_gate_up.shape[-1] // 2
    bm, bn = ROW_BLOCK, 512
    per_seq = seq // bm
    return pl.pallas_call(
        _ffn_up_kernel,
        out_shape=jax.ShapeDtypeStruct((t, f), BF16),
        grid=(t // bm, f // bn),
        in_specs=[
            pl.BlockSpec((bm, d), lambda m, j: (m, 0)),
            pl.BlockSpec((None, 1, d), lambda m, j: (layer, 0, 0)),
            pl.BlockSpec((None, 1, d), lambda m, j: (m // per_seq, 0, 0)),
            pl.BlockSpec((None, 1, d), lambda m, j: (m // per_seq, 0, 0)),
            pl.BlockSpec((None, d, bn), lambda m, j: (li, 0, j)),
            pl.BlockSpec((None, d, bn), lambda m, j: (li, 0, f // bn + j)),
        ],
        out_specs=pl.BlockSpec((bm, bn), lambda m, j: (m, j)),
        scratch_shapes=[pltpu.VMEM((bm, d), BF16)],
        compiler_params=_params("arbitrary", "arbitrary"),
        name="ffn_gate_up",
    )(x2, norm_gain, scale, shift, w_gate_up, w_gate_up)


def _router_kernel(x_ref, gain_ref, scale_ref, shift_ref, r_ref, h_ref, meta_ref, cnt_ref,
                   tri_scr, base_scr, logit_scr, *, n_experts):
    i = pl.program_id(0)
    rows = x_ref.shape[0]
    chunk = 64

    @pl.when(i == 0)
    def _():
        r = lax.broadcasted_iota(jnp.int32, (rows, rows), 0)
        c = lax.broadcasted_iota(jnp.int32, (rows, rows), 1)
        tri_scr[...] = jnp.where(c < r, 1.0, 0.0).astype(BF16)
        base_scr[...] = jnp.zeros(base_scr.shape, F32)

    gain = gain_ref[...]
    one_plus = 1.0 + scale_ref[...]
    shift = shift_ref[...]
    lane_c = lax.broadcasted_iota(jnp.int32, (chunk, LANES), 1)

    def body(c, carry):
        r0 = pl.multiple_of(c * chunk, chunk)
        h = _modulated_norm(x_ref[pl.ds(r0, chunk), :], gain, one_plus, shift)
        h_ref[pl.ds(r0, chunk), :] = h
        lg = jnp.full((chunk, LANES), -jnp.inf, F32)
        for e in range(n_experts):
            col = jnp.sum(h * r_ref[e:e + 1, :], axis=-1, keepdims=True)
            lg = jnp.where(lane_c == e, col, lg)
        logit_scr[pl.ds(r0, chunk), :] = lg
        return carry

    lax.fori_loop(0, rows // chunk, body, 0)

    lg = logit_scr[...]
    lane = lax.broadcasted_iota(jnp.int32, lg.shape, 1).astype(F32)
    m1 = jnp.max(lg, axis=-1, keepdims=True)
    i1 = jnp.min(jnp.where(lg == m1, lane, float(LANES)), axis=-1, keepdims=True)
    lg2 = jnp.where(lane == i1, -jnp.inf, lg)
    m2 = jnp.max(lg2, axis=-1, keepdims=True)
    i2 = jnp.min(jnp.where(lg2 == m2, lane, float(LANES)), axis=-1, keepdims=True)
    ex = jnp.exp(m2 - m1)
    w1 = 1.0 / (1.0 + ex)
    w2 = ex / (1.0 + ex)
    onehot = jnp.where(jnp.logical_or(lane == i1, lane == i2), 1.0, 0.0)
    ranks = jnp.dot(tri_scr[...], onehot.astype(BF16), preferred_element_type=F32) + base_scr[...]
    r1 = jnp.sum(jnp.where(lane == i1, ranks, 0.0), axis=-1, keepdims=True)
    r2 = jnp.sum(jnp.where(lane == i2, ranks, 0.0), axis=-1, keepdims=True)
    meta = jnp.zeros(lg.shape, F32)
    for pos, val in enumerate((i1, i2, r1, r2, w1, w2)):
        meta = jnp.where(lane == float(pos), val, meta)
    meta_ref[...] = meta
    new_base = base_scr[...] + jnp.sum(onehot, axis=0, keepdims=True)
    base_scr[...] = new_base
    cnt_ref[...] = new_base


def _router(x2, norm_gain, scale, shift, w_router_t, layer, seq):
    t, d = x2.shape
    e = w_router_t.shape[0]
    bm = ROUTE_BLOCK
    per_seq = seq // bm
    kern = functools.partial(_router_kernel, n_experts=e)
    return pl.pallas_call(
        kern,
        out_shape=(jax.ShapeDtypeStruct((t, d), F32),
                   jax.ShapeDtypeStruct((t, LANES), F32),
                   jax.ShapeDtypeStruct((1, LANES), F32)),
        grid=(t // bm,),
        in_specs=[
            pl.BlockSpec((bm, d), lambda m: (m, 0)),
            pl.BlockSpec((None, 1, d), lambda m: (layer, 0, 0)),
            pl.BlockSpec((None, 1, d), lambda m: (m // per_seq, 0, 0)),
            pl.BlockSpec((None, 1, d), lambda m: (m // per_seq, 0, 0)),
            pl.BlockSpec((e, d), lambda m: (0, 0)),
        ],
        out_specs=(pl.BlockSpec((bm, d), lambda m: (m, 0)),
                   pl.BlockSpec((bm, LANES), lambda m: (m, 0)),
                   pl.BlockSpec((1, LANES), lambda m: (0, 0))),
        scratch_shapes=[pltpu.VMEM((bm, bm), BF16), pltpu.VMEM((1, LANES), F32),
                        pltpu.VMEM((bm, LANES), F32)],
        compiler_params=_params("arbitrary"),
        name="moe_router",
    )(x2, norm_gain, scale, shift, w_router_t)


def _dispatch_kernel(s1_ref, s2_ref, h_ref, init_ref, xs_ref, sem):
    del init_ref
    rows = h_ref.shape[0]

    def row_copy(i, slot_ref):
        return pltpu.make_async_copy(h_ref.at[pl.ds(i, 1)], xs_ref.at[pl.ds(slot_ref[0, i], 1)], sem)

    def body(i, carry):
        row_copy(i, s1_ref).start()
        row_copy(i, s2_ref).start()
        return carry

    lax.fori_loop(0, rows, body, 0, unroll=8)
    for _ in range(TOP_K):
        pltpu.make_async_copy(h_ref, xs_ref.at[pl.ds(0, rows)], sem).wait()


def _dispatch(h, slot1, slot2, n_slots):
    t, d = h.shape
    bm = ROW_BLOCK
    nblk = t // bm
    slots = [s.reshape(nblk, 1, bm) for s in (slot1, slot2)]
    smem_spec = pl.BlockSpec((None, 1, bm), lambda m: (m, 0, 0), memory_space=pltpu.SMEM)
    return pl.pallas_call(
        _dispatch_kernel,
        out_shape=jax.ShapeDtypeStruct((n_slots, d), F32),
        grid=(nblk,),
        in_specs=[smem_spec, smem_spec,
                  pl.BlockSpec((bm, d), lambda m: (m, 0)),
                  pl.BlockSpec(memory_space=pl.ANY)],
        out_specs=pl.BlockSpec(memory_space=pl.ANY),
        scratch_shapes=[pltpu.SemaphoreType.DMA(())],
        input_output_aliases={3: 0},
        compiler_params=_params("arbitrary"),
        name="moe_dispatch",
    )(slots[0], slots[1], h, jnp.zeros((n_slots, d), F32))


def _moe_up_kernel(be_ref, nl_ref, xs_ref, wg_ref, wu_ref, o_ref, h_scr):
    del be_ref
    live = pl.program_id(0) < nl_ref[0]
    chunk = 128

    @pl.when(jnp.logical_and(live, pl.program_id(1) == 0))
    def _():
        def body(c, carry):
            r0 = pl.multiple_of(c * chunk, chunk)
            h_scr[pl.ds(r0, chunk), :] = xs_ref[pl.ds(r0, chunk), :].astype(h_scr.dtype)
            return carry

        lax.fori_loop(0, h_scr.shape[0] // chunk, body, 0)

    @pl.when(live)
    def _():
        h = h_scr[...]
        g = jnp.dot(h, wg_ref[...].astype(BF16), preferred_element_type=F32)
        u = jnp.dot(h, wu_ref[...].astype(BF16), preferred_element_type=F32)
        o_ref[...] = (_silu(g) * u).astype(o_ref.dtype)

    @pl.when(jnp.logical_not(live))
    def _():
        o_ref[...] = jnp.zeros(o_ref.shape, o_ref.dtype)


def _moe_up(xs, w_gate_up, li, blk_expert, n_live):
    s, d = xs.shape
    f = w_gate_up.shape[-1] // 2
    bm, bn = ROW_BLOCK, 512
    nt = f // bn

    def row(b, j, be, nl):
        return jnp.minimum(b, nl[0] - 1)

    def col(b, j, be, nl):
        return jnp.where(b < nl[0], j, nt - 1)

    grid_spec = pltpu.PrefetchScalarGridSpec(
        num_scalar_prefetch=2,
        grid=(s // bm, nt),
        in_specs=[
            pl.BlockSpec((bm, d), lambda b, j, be, nl: (row(b, j, be, nl), 0)),
            pl.BlockSpec((None, None, d, bn),
                         lambda b, j, be, nl: (li, be[row(b, j, be, nl)], 0, col(b, j, be, nl))),
            pl.BlockSpec((None, None, d, bn),
                         lambda b, j, be, nl: (li, be[row(b, j, be, nl)], 0, nt + col(b, j, be, nl))),
        ],
        out_specs=pl.BlockSpec((bm, bn), lambda b, j, be, nl: (b, j)),
        scratch_shapes=[pltpu.VMEM((bm, d), BF16)],
    )
    return pl.pallas_call(
        _moe_up_kernel,
        out_shape=jax.ShapeDtypeStruct((s, f), BF16),
        grid_spec=grid_spec,
        compiler_params=_params("arbitrary", "arbitrary"),
        name="moe_gate_up",
    )(blk_expert, n_live, xs, w_gate_up, w_gate_up)


def _moe_down_kernel(be_ref, nl_ref, a_ref, w_ref, o_ref):
    del be_ref
    live = pl.program_id(0) < nl_ref[0]

    @pl.when(live)
    def _():
        o_ref[...] = jnp.dot(a_ref[...], w_ref[...].astype(BF16), preferred_element_type=F32)

    @pl.when(jnp.logical_not(live))
    def _():
        o_ref[...] = jnp.zeros(o_ref.shape, o_ref.dtype)


def _moe_down(a, w_down, li, blk_expert, n_live):
    s, f = a.shape
    d = w_down.shape[-1]
    bm, bn = ROW_BLOCK, 256
    nt = d // bn

    def row(b, nl):
        return jnp.minimum(b, nl[0] - 1)

    def col(b, j, nl):
        return jnp.where(b < nl[0], j, nt - 1)

    grid_spec = pltpu.PrefetchScalarGridSpec(
        num_scalar_prefetch=2,
        grid=(s // bm, nt),
        in_specs=[
            pl.BlockSpec((bm, f), lambda b, j, be, nl: (row(b, nl), 0)),
            pl.BlockSpec((None, None, f, bn),
                         lambda b, j, be, nl: (li, be[row(b, nl)], 0, col(b, j, nl))),
        ],
        out_specs=pl.BlockSpec((bm, bn), lambda b, j, be, nl: (b, j)),
    )
    return pl.pallas_call(
        _moe_down_kernel,
        out_shape=jax.ShapeDtypeStruct((s, d), F32),
        grid_spec=grid_spec,
        compiler_params=_params("arbitrary", "arbitrary"),
        name="moe_down",
    )(blk_expert, n_live, a, w_down)


def _combine_kernel(s1_ref, s2_ref, x_ref, gate_ref, meta_ref, ys_ref, o_ref, g1, g2, sem):
    rows = x_ref.shape[0]
    chunk = 128

    def row_copy(i, slot_ref, dst, which):
        return pltpu.make_async_copy(ys_ref.at[pl.ds(slot_ref[0, i], 1)], dst.at[pl.ds(i, 1)],
                                     sem.at[which])

    def body(i, carry):
        row_copy(i, s1_ref, g1, 0).start()
        row_copy(i, s2_ref, g2, 1).start()
        return carry

    lax.fori_loop(0, rows, body, 0, unroll=8)
    pltpu.make_async_copy(ys_ref.at[pl.ds(0, rows)], g1, sem.at[0]).wait()
    pltpu.make_async_copy(ys_ref.at[pl.ds(0, rows)], g2, sem.at[1]).wait()

    gate = gate_ref[...]

    def mix(c, carry):
        r0 = pl.multiple_of(c * chunk, chunk)
        sl = pl.ds(r0, chunk)
        w1 = meta_ref[sl, 4:5]
        w2 = meta_ref[sl, 5:6]
        o_ref[sl, :] = x_ref[sl, :] + gate * (w1 * g1[sl, :] + w2 * g2[sl, :])
        return carry

    lax.fori_loop(0, rows // chunk, mix, 0)


def _combine(ys, slot1, slot2, meta, x2, gate, seq):
    t, d = x2.shape
    bm = ROUTE_BLOCK
    nblk = t // bm
    per_seq = seq // bm
    slots = [s.reshape(nblk, 1, bm) for s in (slot1, slot2)]
    smem_spec = pl.BlockSpec((None, 1, bm), lambda m: (m, 0, 0), memory_space=pltpu.SMEM)
    return pl.pallas_call(
        _combine_kernel,
        out_shape=jax.ShapeDtypeStruct((t, d), F32),
        grid=(nblk,),
        in_specs=[smem_spec, smem_spec,
                  pl.BlockSpec((bm, d), lambda m: (m, 0)),
                  pl.BlockSpec((None, 1, d), lambda m: (m // per_seq, 0, 0)),
                  pl.BlockSpec((bm, LANES), lambda m: (m, 0)),
                  pl.BlockSpec(memory_space=pl.ANY)],
        out_specs=pl.BlockSpec((bm, d), lambda m: (m, 0)),
        scratch_shapes=[pltpu.VMEM((bm, d), F32), pltpu.VMEM((bm, d), F32),
                        pltpu.SemaphoreType.DMA((2,))],
        compiler_params=_params("arbitrary"),
        name="moe_combine",
    )(slots[0], slots[1], x2, gate, meta, ys)


def _moe(x2, norm_gain, scale, shift, gate, w_router, w_gate_up, w_down, layer, li, seq):
    t, d = x2.shape
    e = w_router.shape[-1]
    blk = ROW_BLOCK
    h, meta, counts = _router(x2, norm_gain, scale, shift, jnp.transpose(w_router[li]), layer, seq)

    cnt = counts[0, :e].astype(jnp.int32)
    nblk = (cnt + blk - 1) // blk
    blk_end = jnp.cumsum(nblk)
    off = (blk_end - nblk) * blk
    n_blocks = TOP_K * t // blk + e
    n_live = blk_end[-1:].astype(jnp.int32)
    block_ids = jnp.arange(n_blocks, dtype=jnp.int32)
    blk_expert = jnp.minimum(jnp.sum(block_ids[:, None] >= blk_end[None, :], axis=1), e - 1).astype(jnp.int32)
    ids = meta[:, 0:2].astype(jnp.int32)
    ranks = meta[:, 2:4].astype(jnp.int32)
    slots = off[ids] + ranks
    slot1, slot2 = slots[:, 0], slots[:, 1]

    xs = _dispatch(h, slot1, slot2, n_blocks * blk)
    a = _moe_up(xs, w_gate_up, li, blk_expert, n_live)
    ys = _moe_down(a, w_down, li, blk_expert, n_live)
    return _combine(ys, slot1, slot2, meta, x2, gate, seq)


def kernel(x, c, ada_w, ada_b, norm_mix, w_in, q_norm, k_norm, attn_sinks, conv_w, w_attn_branch,
           w_conv_branch, w_out, norm_ffn, ffn_w_gate_up, ffn_w_down, moe_w_router, moe_w_gate_up,
           moe_w_down):
    batch, seq, d = x.shape
    depth = ada_w.shape[0]
    d_attn = w_attn_branch.shape[1]
    d_conv = w_conv_branch.shape[1]
    n_heads = d_attn // HEAD_DIM
    d_kv = (n_heads // GQA_GROUP) * HEAD_DIM
    col_b = d_attn + 2 * d_kv
    col_ga = col_b + 3 * d_conv
    col_gc = col_ga + d
    assert w_in.shape[-1] == col_gc + d and seq % ROW_BLOCK == 0
    assert 2 * HEAD_DIM == LANES and GQA_GROUP == 4 and (n_heads // GQA_GROUP) % 2 == 0

    n_mod = ada_w.shape[-1] // d
    mod = _adaln(c, ada_w, ada_b).reshape(depth, batch, n_mod, 1, d)
    bias = _alibi_bias(n_heads)
    x2 = x.reshape(batch * seq, d)

    for l in range(depth):
        shift_m, scale_m, gate_m, shift_f, scale_f, gate_f = (mod[l, :, i] for i in range(n_mod))

        reps_q, reps_k = d_attn // HEAD_DIM, d_kv // HEAD_DIM
        head_gain = jnp.concatenate([jnp.tile(q_norm[l] * HEAD_DIM ** -0.5, reps_q),
                                     jnp.tile(k_norm[l], reps_k), jnp.ones((d_kv,), F32)])[None]
        head_mask = jnp.concatenate([jnp.ones((d_attn + d_kv,), F32), jnp.zeros((d_kv,), F32)])[None]

        y = _inproj(x2, norm_mix.reshape(depth, 1, d), scale_m, shift_m, w_in, l, head_gain, head_mask, seq)
        attn = _attention(y, attn_sinks[l], bias, batch, seq, d_attn, d_kv)
        merged = _branch_merge(attn, y, conv_w, w_attn_branch, w_conv_branch, l, seq, col_b, col_ga, col_gc)
        x2 = _resid_proj(merged, w_out, l, x2, gate_m, seq, 512, "mixer_out_proj")

        norm_f = norm_ffn.reshape(depth, 1, d)
        if l % 2 == 0:
            a = _ffn_up(x2, norm_f, scale_f, shift_f, ffn_w_gate_up, l, l // 2, seq)
            x2 = _resid_proj(a, ffn_w_down, l // 2, x2, gate_f, seq, 256, "ffn_down")
        else:
            x2 = _moe(x2, norm_f, scale_f, shift_f, gate_f, moe_w_router, moe_w_gate_up, moe_w_down,
                      l, l // 2, seq)
    return x2.reshape(batch, seq, d)
```

```python
import functools

import jax
import jax.numpy as jnp
from jax import lax
from jax.experimental import pallas as pl
from jax.experimental.pallas import tpu as pltpu

F32 = jnp.float32
BF16 = jnp.bfloat16

HEAD_DIM = 64
GQA_GROUP = 4
ATTN_BLOCK = 128
TOP_K = 2
EPS = 1e-6

LANES = 128
BF16_ROWS = 16
V7X_VMEM_BYTES = 64 * 1024 * 1024
VMEM_LIMIT = V7X_VMEM_BYTES - 8 * 1024 * 1024

ROW_BLOCK = 1024
ROUTE_BLOCK = 512
MOE_BLOCK = 1056
MOE_CHUNK = 96


def _params(*sem):
    return pltpu.CompilerParams(dimension_semantics=sem, vmem_limit_bytes=VMEM_LIMIT)


def _silu(x):
    return x * jax.nn.sigmoid(x)


def _modulated_norm(x, gain, one_plus_scale, shift):
    ms = jnp.mean(x * x, axis=-1, keepdims=True)
    y = (x * lax.rsqrt(ms + EPS)) * gain
    return y * one_plus_scale + shift


def _fill_modulated_norm(x_ref, gain_ref, scale_ref, shift_ref, h_ref, chunk=128):
    gain = gain_ref[...]
    one_plus = 1.0 + scale_ref[...]
    shift = shift_ref[...]

    def body(c, carry):
        r0 = pl.multiple_of(c * chunk, chunk)
        h = _modulated_norm(x_ref[pl.ds(r0, chunk), :], gain, one_plus, shift)
        h_ref[pl.ds(r0, chunk), :] = h.astype(h_ref.dtype)
        return carry

    lax.fori_loop(0, x_ref.shape[0] // chunk, body, 0)


def _adaln_kernel(c_ref, w_ref, b_ref, o_ref):
    s = _silu(c_ref[...]).astype(BF16)
    o_ref[...] = jnp.dot(s, w_ref[...].astype(BF16), preferred_element_type=F32) + b_ref[...]


def _adaln(c, ada_w, ada_b):
    depth, d, n = ada_w.shape
    b = c.shape[0]
    rows = 8 * pl.cdiv(b, 8)
    c_pad = jnp.pad(c, ((0, rows - b), (0, 0)))
    bn = 1024
    out = pl.pallas_call(
        _adaln_kernel,
        out_shape=jax.ShapeDtypeStruct((depth, rows, n), F32),
        grid=(depth, n // bn),
        in_specs=[
            pl.BlockSpec((rows, d), lambda l, j: (0, 0)),
            pl.BlockSpec((None, d, bn), lambda l, j: (l, 0, j)),
            pl.BlockSpec((None, 1, bn), lambda l, j: (l, 0, j)),
        ],
        out_specs=pl.BlockSpec((None, rows, bn), lambda l, j: (l, 0, j)),
        compiler_params=_params("arbitrary", "arbitrary"),
        name="adaln_mod",
    )(c_pad, ada_w, ada_b.reshape(depth, 1, n))
    return out[:, :b]


def _modnorm_kernel(x_ref, gain_ref, scale_ref, shift_ref, o_ref):
    _fill_modulated_norm(x_ref, gain_ref, scale_ref, shift_ref, o_ref)


def _modnorm(x2, norm_gain, scale, shift, layer, seq):
    t, d = x2.shape
    bm = ROUTE_BLOCK
    per_seq = seq // bm
    return pl.pallas_call(
        _modnorm_kernel,
        out_shape=jax.ShapeDtypeStruct((t, d), BF16),
        grid=(t // bm,),
        in_specs=[
            pl.BlockSpec((bm, d), lambda m: (m, 0)),
            pl.BlockSpec((None, 1, d), lambda m: (layer, 0, 0)),
            pl.BlockSpec((None, 1, d), lambda m: (m // per_seq, 0, 0)),
            pl.BlockSpec((None, 1, d), lambda m: (m // per_seq, 0, 0)),
        ],
        out_specs=pl.BlockSpec((bm, d), lambda m: (m, 0)),
        compiler_params=_params("arbitrary"),
        name="mixer_modnorm",
    )(x2, norm_gain, scale, shift)


def _inproj_kernel(h_ref, w_ref, hg_ref, hm_ref, o_ref, *, norm_tiles):
    n = pl.program_id(1)
    acc = jnp.dot(h_ref[...], w_ref[...].astype(BF16), preferred_element_type=F32)

    @pl.when(n < norm_tiles)
    def _():
        bn = acc.shape[1]
        shift = HEAD_DIM.bit_length() - 1
        r = lax.shift_right_logical(lax.broadcasted_iota(jnp.int32, (bn, bn), 0), shift)
        c = lax.shift_right_logical(lax.broadcasted_iota(jnp.int32, (bn, bn), 1), shift)
        head_mean = jnp.where(r == c, 1.0 / HEAD_DIM, 0.0).astype(BF16)
        ms = jnp.dot((acc * acc).astype(BF16), head_mean, preferred_element_type=F32)
        normed = acc * lax.rsqrt(ms + EPS) * hg_ref[...]
        o_ref[...] = jnp.where(hm_ref[...] > 0.0, normed, acc).astype(o_ref.dtype)

    @pl.when(n >= norm_tiles)
    def _():
        o_ref[...] = acc.astype(o_ref.dtype)


def _inproj(h, w_in, layer, head_gain, head_mask):
    t, d = h.shape
    n = w_in.shape[-1]
    bm, bn = 2 * ROW_BLOCK, 512
    norm_cols = head_gain.shape[-1]
    kern = functools.partial(_inproj_kernel, norm_tiles=norm_cols // bn)
    return pl.pallas_call(
        kern,
        out_shape=jax.ShapeDtypeStruct((t, n), BF16),
        grid=(t // bm, n // bn),
        in_specs=[
            pl.BlockSpec((bm, d), lambda m, j: (m, 0)),
            pl.BlockSpec((None, d, bn), lambda m, j: (layer, 0, j)),
            pl.BlockSpec((1, bn), lambda m, j: (0, jnp.minimum(j, norm_cols // bn - 1))),
            pl.BlockSpec((1, bn), lambda m, j: (0, jnp.minimum(j, norm_cols // bn - 1))),
        ],
        out_specs=pl.BlockSpec((bm, bn), lambda m, j: (m, j)),
        compiler_params=_params("arbitrary", "arbitrary"),
        name="mixer_in_proj",
    )(h, w_in, head_gain, head_mask)


def _attn_kernel(sinks_ref, q_ref, kvp_ref, kvc_ref, bias_ref, o_ref, *, n_kv):
    blk = ATTN_BLOCK
    first = pl.program_id(1) == 0
    dkv = n_kv * HEAD_DIM
    q = q_ref[...].astype(F32)
    kvp = kvp_ref[...].astype(F32)
    kvc = kvc_ref[...].astype(F32)
    k_all = jnp.concatenate([kvp[:, :dkv], kvc[:, :dkv]], axis=0)
    v_all = jnp.concatenate([kvp[:, dkv:], kvc[:, dkv:]], axis=0)
    lo = lax.broadcasted_iota(jnp.int32, (1, LANES), 1) < HEAD_DIM
    col = lax.broadcasted_iota(jnp.int32, (1, 2 * blk), 1)
    no_prev = jnp.where(jnp.logical_and(first, col < blk), -jnp.inf, 0.0).astype(F32)

    for j in range(n_kv):
        t, half = divmod(j, 2)
        kp = k_all[:, LANES * t:LANES * (t + 1)]
        vp = v_all[:, LANES * t:LANES * (t + 1)]
        kr = pltpu.roll(kp, HEAD_DIM, 1)
        vr = pltpu.roll(vp, HEAD_DIM, 1)
        k_lo = jnp.where(lo, kr if half else kp, 0.0).astype(BF16)
        v_lo = jnp.where(lo, vr if half else vp, 0.0)
        v_hi = jnp.where(lo, 0.0, vp if half else vr)
        v_stack = jnp.concatenate([v_lo, v_hi], axis=0).astype(BF16)
        base = GQA_GROUP * HEAD_DIM * j
        qa = q[:, base:base + LANES]
        qb = q[:, base + LANES:base + 2 * LANES]
        q_stack = jnp.concatenate(
            [qa, pltpu.roll(qa, HEAD_DIM, 1), qb, pltpu.roll(qb, HEAD_DIM, 1)], axis=0).astype(BF16)
        s = lax.dot_general(q_stack, k_lo, (((1,), (1,)), ((), ())),
                            preferred_element_type=F32)
        probs, inv = [], []
        for g in range(GQA_GROUP):
            h = GQA_GROUP * j + g
            sg = s[blk * g:blk * (g + 1)] + (bias_ref[h] + no_prev)
            sink = sinks_ref[h]
            m = jnp.maximum(jnp.max(sg, axis=-1, keepdims=True), sink)
            p = jnp.exp(sg - m)
            denom = jnp.sum(p, axis=-1, keepdims=True) + jnp.exp(sink - m)
            probs.append(p.astype(BF16))
            inv.append(1.0 / denom)
        for pair in range(GQA_GROUP // 2):
            pp = jnp.concatenate([probs[2 * pair], probs[2 * pair + 1]], axis=1)
            o = jnp.dot(pp, v_stack, preferred_element_type=F32)
            o = o * jnp.where(lo, inv[2 * pair], inv[2 * pair + 1])
            o_ref[:, base + LANES * pair:base + LANES * (pair + 1)] = o.astype(o_ref.dtype)


def _alibi_bias(n_heads):
    blk = ATTN_BLOCK
    slopes = 2.0 ** (-8.0 * jnp.arange(1, n_heads + 1, dtype=F32) / n_heads)
    dist = (jnp.arange(blk) + blk)[:, None] - jnp.arange(2 * blk)[None, :]
    valid = (dist >= 0) & (dist < blk)
    bias = -(slopes[:, None, None] * dist.astype(F32)[None])
    return jnp.where(valid[None], bias, -jnp.inf)


def _attention(y, sinks, bias, batch, seq, d_attn, d_kv):
    t = y.shape[0]
    blk = ATTN_BLOCK
    nb = seq // blk
    kv_w = 2 * d_kv
    kv_col = d_attn // kv_w
    kern = functools.partial(_attn_kernel, n_kv=d_kv // HEAD_DIM)
    grid_spec = pltpu.PrefetchScalarGridSpec(
        num_scalar_prefetch=1,
        grid=(batch, nb),
        in_specs=[
            pl.BlockSpec((blk, d_attn), lambda b, i, s: (b * nb + i, 0)),
            pl.BlockSpec((blk, kv_w), lambda b, i, s: (b * nb + jnp.maximum(i - 1, 0), kv_col)),
            pl.BlockSpec((blk, kv_w), lambda b, i, s: (b * nb + i, kv_col)),
            pl.BlockSpec(bias.shape, lambda b, i, s: (0, 0, 0)),
        ],
        out_specs=pl.BlockSpec((blk, d_attn), lambda b, i, s: (b * nb + i, 0)),
    )
    return pl.pallas_call(
        kern,
        out_shape=jax.ShapeDtypeStruct((t, d_attn), BF16),
        grid_spec=grid_spec,
        compiler_params=_params("arbitrary", "arbitrary"),
        name="swa_attention",
    )(sinks, y, y, y, bias)


HALO = BF16_ROWS


def _branch_kernel(attn_ref, bc_ref, cx_ref, hbc_ref, hcx_ref, cw_ref, wa_ref, wc_ref, ga_ref, gc_ref,
                   o_ref, u_scr, conv_scr, *, blocks_per_seq, taps):
    m = pl.program_id(0)
    n = pl.program_id(1)
    dc = conv_scr.shape[1]
    half = dc // 2
    chunk = 64

    @pl.when(n == 0)
    def _():
        keep = jnp.where(m % blocks_per_seq == 0, 0.0, 1.0).astype(F32)
        hc = jnp.concatenate([hbc_ref[:, dc:], hcx_ref[:, :half]], axis=1).astype(F32)
        hx = hcx_ref[:, half:].astype(F32)
        u_scr[0:HALO, :] = hc * hx * keep

        def fill_u(c, carry):
            r0 = pl.multiple_of(c * chunk, chunk)
            cc = jnp.concatenate([bc_ref[pl.ds(r0, chunk), dc:], cx_ref[pl.ds(r0, chunk), :half]], axis=1)
            xx = cx_ref[pl.ds(r0, chunk), half:]
            u_scr[pl.ds(HALO + r0, chunk), :] = cc.astype(F32) * xx.astype(F32)
            return carry

        lax.fori_loop(0, conv_scr.shape[0] // chunk, fill_u, 0)

        def conv_rows(c, carry):
            r0 = pl.multiple_of(c * chunk, chunk)
            w0 = pl.multiple_of(r0 + HALO - 8, 8)
            win = u_scr[pl.ds(w0, chunk + 8), :]
            acc = cw_ref[taps - 1:taps, :] * win[8:]
            for back in range(1, taps):
                acc = acc + cw_ref[taps - 1 - back:taps - back, :] * pltpu.roll(win, back, 0)[8:]
            b = bc_ref[pl.ds(r0, chunk), :dc].astype(F32)
            conv_scr[pl.ds(r0, chunk), :] = (b * acc).astype(conv_scr.dtype)
            return carry

        lax.fori_loop(0, conv_scr.shape[0] // chunk, conv_rows, 0)

    a = jnp.dot(attn_ref[...], wa_ref[...].astype(BF16), preferred_element_type=F32)
    c = jnp.dot(conv_scr[...], wc_ref[...].astype(BF16), preferred_element_type=F32)
    merged = jax.nn.sigmoid(ga_ref[...].astype(F32)) * a + jax.nn.sigmoid(gc_ref[...].astype(F32)) * c
    o_ref[...] = merged.astype(o_ref.dtype)


def _branch_merge(attn, y, conv_w, w_attn, w_conv, layer, seq, col_b, col_ga, col_gc):
    t, d_attn = attn.shape
    dc = w_conv.shape[1]
    d = w_attn.shape[-1]
    taps = conv_w.shape[1]
    bm, bn = ROW_BLOCK, 512
    wide = dc + dc // 2
    assert col_b % wide == 0 and taps - 1 <= 8
    cb = col_b // wide
    halo_blocks = bm // HALO
    kern = functools.partial(_branch_kernel, blocks_per_seq=seq // bm, taps=taps)

    def halo_map(col):
        return lambda m, j: (jnp.maximum(m * halo_blocks - 1, 0), col)

    return pl.pallas_call(
        kern,
        out_shape=jax.ShapeDtypeStruct((t, d), BF16),
        grid=(t // bm, d // bn),
        in_specs=[
            pl.BlockSpec((bm, d_attn), lambda m, j: (m, 0)),
            pl.BlockSpec((bm, wide), lambda m, j: (m, cb)),
            pl.BlockSpec((bm, wide), lambda m, j: (m, cb + 1)),
            pl.BlockSpec((HALO, wide), halo_map(cb)),
            pl.BlockSpec((HALO, wide), halo_map(cb + 1)),
            pl.BlockSpec((None, taps, dc), lambda m, j: (layer, 0, 0)),
            pl.BlockSpec((None, d_attn, bn), lambda m, j: (layer, 0, j)),
            pl.BlockSpec((None, dc, bn), lambda m, j: (layer, 0, j)),
            pl.BlockSpec((bm, bn), lambda m, j: (m, col_ga // bn + j)),
            pl.BlockSpec((bm, bn), lambda m, j: (m, col_gc // bn + j)),
        ],
        out_specs=pl.BlockSpec((bm, bn), lambda m, j: (m, j)),
        scratch_shapes=[pltpu.VMEM((bm + HALO, dc), F32), pltpu.VMEM((bm, dc), BF16)],
        compiler_params=_params("arbitrary", "arbitrary"),
        name="branch_merge",
    )(attn, y, y, y, y, conv_w, w_attn, w_conv, y, y)


def _resid_proj_kernel(lhs_ref, w_ref, x_ref, gate_ref, o_ref):
    acc = jnp.dot(lhs_ref[...], w_ref[...].astype(BF16), preferred_element_type=F32)
    o_ref[...] = x_ref[...] + gate_ref[...] * acc


def _resid_proj(lhs, w, layer, x2, gate, seq, bn, name):
    t, k = lhs.shape
    d = w.shape[-1]
    bm = ROW_BLOCK
    per_seq = seq // bm
    return pl.pallas_call(
        _resid_proj_kernel,
        out_shape=jax.ShapeDtypeStruct((t, d), F32),
        grid=(t // bm, d // bn),
        in_specs=[
            pl.BlockSpec((bm, k), lambda m, j: (m, 0)),
            pl.BlockSpec((None, k, bn), lambda m, j: (layer, 0, j)),
            pl.BlockSpec((bm, bn), lambda m, j: (m, j)),
            pl.BlockSpec((None, 1, bn), lambda m, j: (m // per_seq, 0, j)),
        ],
        out_specs=pl.BlockSpec((bm, bn), lambda m, j: (m, j)),
        compiler_params=_params("arbitrary", "arbitrary"),
        name=name,
    )(lhs, w, x2, gate)


def _ffn_up_kernel(x_ref, gain_ref, scale_ref, shift_ref, wg_ref, wu_ref, o_ref, h_scr):
    @pl.when(pl.program_id(1) == 0)
    def _():
        _fill_modulated_norm(x_ref, gain_ref, scale_ref, shift_ref, h_scr)

    h = h_scr[...]
    g = jnp.dot(h, wg_ref[...].astype(BF16), preferred_element_type=F32)
    u = jnp.dot(h, wu_ref[...].astype(BF16), preferred_element_type=F32)
    o_ref[...] = (_silu(g) * u).astype(o_ref.dtype)


def _ffn_up(x2, norm_gain, scale, shift, w_gate_up, layer, li, seq):
    t, d = x2.shape
    f = w_gate_up.shape[-1] // 2
    bm, bn = ROW_BLOCK, 512
    per_seq = seq // bm
    return pl.pallas_call(
        _ffn_up_kernel,
        out_shape=jax.ShapeDtypeStruct((t, f), BF16),
        grid=(t // bm, f // bn),
        in_specs=[
            pl.BlockSpec((bm, d), lambda m, j: (m, 0)),
            pl.BlockSpec((None, 1, d), lambda m, j: (layer, 0, 0)),
            pl.BlockSpec((None, 1, d), lambda m, j: (m // per_seq, 0, 0)),
            pl.BlockSpec((None, 1, d), lambda m, j: (m // per_seq, 0, 0)),
            pl.BlockSpec((None, d, bn), lambda m, j: (li, 0, j)),
            pl.BlockSpec((None, d, bn), lambda m, j: (li, 0, f // bn + j)),
        ],
        out_specs=pl.BlockSpec((bm, bn), lambda m, j: (m, j)),
        scratch_shapes=[pltpu.VMEM((bm, d), BF16)],
        compiler_params=_params("arbitrary", "arbitrary"),
        name="ffn_gate_up",
    )(x2, norm_gain, scale, shift, w_gate_up, w_gate_up)


def _router_kernel(x_ref, gain_ref, scale_ref, shift_ref, r_ref, h_ref, meta_ref, cnt_ref,
                   tri_scr, base_scr, logit_scr, *, n_experts):
    i = pl.program_id(0)
    rows = x_ref.shape[0]
    chunk = 64

    @pl.when(i == 0)
    def _():
        r = lax.broadcasted_iota(jnp.int32, (rows, rows), 0)
        c = lax.broadcasted_iota(jnp.int32, (rows, rows), 1)
        tri_scr[...] = jnp.where(c < r, 1.0, 0.0).astype(BF16)
        base_scr[...] = jnp.zeros(base_scr.shape, F32)

    gain = gain_ref[...]
    one_plus = 1.0 + scale_ref[...]
    shift = shift_ref[...]
    lane_c = lax.broadcasted_iota(jnp.int32, (chunk, LANES), 1)

    def body(c, carry):
        r0 = pl.multiple_of(c * chunk, chunk)
        h = _modulated_norm(x_ref[pl.ds(r0, chunk), :], gain, one_plus, shift)
        h_ref[pl.ds(r0, chunk), :] = h
        lg = jnp.full((chunk, LANES), -jnp.inf, F32)
        for e in range(n_experts):
            col = jnp.sum(h * r_ref[e:e + 1, :], axis=-1, keepdims=True)
            lg = jnp.where(lane_c == e, col, lg)
        logit_scr[pl.ds(r0, chunk), :] = lg
        return carry

    lax.fori_loop(0, rows // chunk, body, 0)

    lg = logit_scr[...]
    lane = lax.broadcasted_iota(jnp.int32, lg.shape, 1).astype(F32)
    m1 = jnp.max(lg, axis=-1, keepdims=True)
    i1 = jnp.min(jnp.where(lg == m1, lane, float(LANES)), axis=-1, keepdims=True)
    lg2 = jnp.where(lane == i1, -jnp.inf, lg)
    m2 = jnp.max(lg2, axis=-1, keepdims=True)
    i2 = jnp.min(jnp.where(lg2 == m2, lane, float(LANES)), axis=-1, keepdims=True)
    ex = jnp.exp(m2 - m1)
    w1 = 1.0 / (1.0 + ex)
    w2 = ex / (1.0 + ex)
    onehot = jnp.where(jnp.logical_or(lane == i1, lane == i2), 1.0, 0.0)
    ranks = jnp.dot(tri_scr[...], onehot.astype(BF16), preferred_element_type=F32) + base_scr[...]
    r1 = jnp.sum(jnp.where(lane == i1, ranks, 0.0), axis=-1, keepdims=True)
    r2 = jnp.sum(jnp.where(lane == i2, ranks, 0.0), axis=-1, keepdims=True)
    meta = jnp.zeros(lg.shape, F32)
    for pos, val in enumerate((i1, i2, r1, r2, w1, w2)):
        meta = jnp.where(lane == float(pos), val, meta)
    meta_ref[...] = meta
    new_base = base_scr[...] + jnp.sum(onehot, axis=0, keepdims=True)
    base_scr[...] = new_base
    cnt_ref[...] = new_base


def _router(x2, norm_gain, scale, shift, w_router_t, layer, seq):
    t, d = x2.shape
    e = w_router_t.shape[0]
    bm = ROUTE_BLOCK
    per_seq = seq // bm
    kern = functools.partial(_router_kernel, n_experts=e)
    return pl.pallas_call(
        kern,
        out_shape=(jax.ShapeDtypeStruct((t, d), F32),
                   jax.ShapeDtypeStruct((t, LANES), F32),
                   jax.ShapeDtypeStruct((1, LANES), F32)),
        grid=(t // bm,),
        in_specs=[
            pl.BlockSpec((bm, d), lambda m: (m, 0)),
            pl.BlockSpec((None, 1, d), lambda m: (layer, 0, 0)),
            pl.BlockSpec((None, 1, d), lambda m: (m // per_seq, 0, 0)),
            pl.BlockSpec((None, 1, d), lambda m: (m // per_seq, 0, 0)),
            pl.BlockSpec((e, d), lambda m: (0, 0)),
        ],
        out_specs=(pl.BlockSpec((bm, d), lambda m: (m, 0)),
                   pl.BlockSpec((bm, LANES), lambda m: (m, 0)),
                   pl.BlockSpec((1, LANES), lambda m: (0, 0))),
        scratch_shapes=[pltpu.VMEM((bm, bm), BF16), pltpu.VMEM((1, LANES), F32),
                        pltpu.VMEM((bm, LANES), F32)],
        compiler_params=_params("arbitrary"),
        name="moe_router",
    )(x2, norm_gain, scale, shift, w_router_t)


def _dispatch_kernel(s1_ref, s2_ref, h_ref, init_ref, xs_ref, sem):
    del init_ref
    rows = h_ref.shape[0]

    def row_copy(i, slot_ref):
        return pltpu.make_async_copy(h_ref.at[pl.ds(i, 1)], xs_ref.at[pl.ds(slot_ref[0, i], 1)], sem)

    def body(i, carry):
        row_copy(i, s1_ref).start(priority=0)
        row_copy(i, s2_ref).start(priority=1)
        return carry

    lax.fori_loop(0, rows, body, 0, unroll=8)
    for _ in range(TOP_K):
        pltpu.make_async_copy(h_ref, xs_ref.at[pl.ds(0, rows)], sem).wait()


def _dispatch(h, slot1, slot2, n_slots):
    t, d = h.shape
    bm = ROW_BLOCK
    nblk = t // bm
    slots = [s.reshape(nblk, 1, bm) for s in (slot1, slot2)]
    smem_spec = pl.BlockSpec((None, 1, bm), lambda m: (m, 0, 0), memory_space=pltpu.SMEM)
    return pl.pallas_call(
        _dispatch_kernel,
        out_shape=jax.ShapeDtypeStruct((n_slots, d), F32),
        grid=(nblk,),
        in_specs=[smem_spec, smem_spec,
                  pl.BlockSpec((bm, d), lambda m: (m, 0)),
                  pl.BlockSpec(memory_space=pl.ANY)],
        out_specs=pl.BlockSpec(memory_space=pl.ANY),
        scratch_shapes=[pltpu.SemaphoreType.DMA(())],
        input_output_aliases={3: 0},
        compiler_params=_params("arbitrary"),
        name="moe_dispatch",
    )(slots[0], slots[1], h, jnp.zeros((n_slots, d), F32))


def _moe_up_kernel(be_ref, nl_ref, xs_ref, wg_ref, wu_ref, o_ref, h_scr):
    del be_ref
    live = pl.program_id(0) < nl_ref[0]
    chunk = MOE_CHUNK

    @pl.when(jnp.logical_and(live, pl.program_id(1) == 0))
    def _():
        def body(c, carry):
            r0 = pl.multiple_of(c * chunk, chunk)
            h_scr[pl.ds(r0, chunk), :] = xs_ref[pl.ds(r0, chunk), :].astype(h_scr.dtype)
            return carry

        lax.fori_loop(0, h_scr.shape[0] // chunk, body, 0)

    @pl.when(live)
    def _():
        h = h_scr[...]
        g = jnp.dot(h, wg_ref[...].astype(BF16), preferred_element_type=F32)
        u = jnp.dot(h, wu_ref[...].astype(BF16), preferred_element_type=F32)
        o_ref[...] = (_silu(g) * u).astype(o_ref.dtype)

    @pl.when(jnp.logical_not(live))
    def _():
        o_ref[...] = jnp.zeros(o_ref.shape, o_ref.dtype)


def _moe_up(xs, w_gate_up, li, blk_expert, n_live):
    s, d = xs.shape
    f = w_gate_up.shape[-1] // 2
    bm, bn = MOE_BLOCK, 512
    nt = f // bn

    def row(b, j, be, nl):
        return jnp.minimum(b, nl[0] - 1)

    def col(b, j, be, nl):
        return jnp.where(b < nl[0], j, nt - 1)

    grid_spec = pltpu.PrefetchScalarGridSpec(
        num_scalar_prefetch=2,
        grid=(s // bm, nt),
        in_specs=[
            pl.BlockSpec((bm, d), lambda b, j, be, nl: (row(b, j, be, nl), 0)),
            pl.BlockSpec((None, None, d, bn),
                         lambda b, j, be, nl: (li, be[row(b, j, be, nl)], 0, col(b, j, be, nl))),
            pl.BlockSpec((None, None, d, bn),
                         lambda b, j, be, nl: (li, be[row(b, j, be, nl)], 0, nt + col(b, j, be, nl))),
        ],
        out_specs=pl.BlockSpec((bm, bn), lambda b, j, be, nl: (b, j)),
        scratch_shapes=[pltpu.VMEM((bm, d), BF16)],
    )
    return pl.pallas_call(
        _moe_up_kernel,
        out_shape=jax.ShapeDtypeStruct((s, f), BF16),
        grid_spec=grid_spec,
        compiler_params=_params("arbitrary", "arbitrary"),
        name="moe_gate_up",
    )(blk_expert, n_live, xs, w_gate_up, w_gate_up)


def _moe_down_kernel(be_ref, nl_ref, a_ref, w_ref, o_ref):
    del be_ref
    live = pl.program_id(0) < nl_ref[0]

    @pl.when(live)
    def _():
        o_ref[...] = jnp.dot(a_ref[...], w_ref[...].astype(BF16), preferred_element_type=F32)

    @pl.when(jnp.logical_not(live))
    def _():
        o_ref[...] = jnp.zeros(o_ref.shape, o_ref.dtype)


def _moe_down(a, w_down, li, blk_expert, n_live):
    s, f = a.shape
    d = w_down.shape[-1]
    bm, bn = MOE_BLOCK, 256
    nt = d // bn

    def row(b, nl):
        return jnp.minimum(b, nl[0] - 1)

    def col(b, j, nl):
        return jnp.where(b < nl[0], j, nt - 1)

    grid_spec = pltpu.PrefetchScalarGridSpec(
        num_scalar_prefetch=2,
        grid=(s // bm, nt),
        in_specs=[
            pl.BlockSpec((bm, f), lambda b, j, be, nl: (row(b, nl), 0)),
            pl.BlockSpec((None, None, f, bn),
                         lambda b, j, be, nl: (li, be[row(b, nl)], 0, col(b, j, nl))),
        ],
        out_specs=pl.BlockSpec((bm, bn), lambda b, j, be, nl: (b, j)),
    )
    return pl.pallas_call(
        _moe_down_kernel,
        out_shape=jax.ShapeDtypeStruct((s, d), F32),
        grid_spec=grid_spec,
        compiler_params=_params("arbitrary", "arbitrary"),
        name="moe_down",
    )(blk_expert, n_live, a, w_down)


def _combine_kernel(s1_ref, s2_ref, x_ref, gate_ref, meta_ref, ys_ref, o_ref, g1, g2, sem):
    rows = x_ref.shape[0]
    chunk = 128

    def row_copy(i, slot_ref, dst, which):
        return pltpu.make_async_copy(ys_ref.at[pl.ds(slot_ref[0, i], 1)], dst.at[pl.ds(i, 1)],
                                     sem.at[which])

    def body(i, carry):
        row_copy(i, s1_ref, g1, 0).start(priority=0)
        row_copy(i, s2_ref, g2, 1).start(priority=1)
        return carry

    lax.fori_loop(0, rows, body, 0, unroll=8)
    pltpu.make_async_copy(ys_ref.at[pl.ds(0, rows)], g1, sem.at[0]).wait()
    pltpu.make_async_copy(ys_ref.at[pl.ds(0, rows)], g2, sem.at[1]).wait()

    gate = gate_ref[...]

    def mix(c, carry):
        r0 = pl.multiple_of(c * chunk, chunk)
        sl = pl.ds(r0, chunk)
        w1 = meta_ref[sl, 4:5]
        w2 = meta_ref[sl, 5:6]
        o_ref[sl, :] = x_ref[sl, :] + gate * (w1 * g1[sl, :] + w2 * g2[sl, :])
        return carry

    lax.fori_loop(0, rows // chunk, mix, 0)


def _combine(ys, slot1, slot2, meta, x2, gate, seq):
    t, d = x2.shape
    bm = ROUTE_BLOCK
    nblk = t // bm
    per_seq = seq // bm
    slots = [s.reshape(nblk, 1, bm) for s in (slot1, slot2)]
    smem_spec = pl.BlockSpec((None, 1, bm), lambda m: (m, 0, 0), memory_space=pltpu.SMEM)
    return pl.pallas_call(
        _combine_kernel,
        out_shape=jax.ShapeDtypeStruct((t, d), F32),
        grid=(nblk,),
        in_specs=[smem_spec, smem_spec,
                  pl.BlockSpec((bm, d), lambda m: (m, 0)),
                  pl.BlockSpec((None, 1, d), lambda m: (m // per_seq, 0, 0)),
                  pl.BlockSpec((bm, LANES), lambda m: (m, 0)),
                  pl.BlockSpec(memory_space=pl.ANY)],
        out_specs=pl.BlockSpec((bm, d), lambda m: (m, 0)),
        scratch_shapes=[pltpu.VMEM((bm, d), F32), pltpu.VMEM((bm, d), F32),
                        pltpu.SemaphoreType.DMA((2,))],
        compiler_params=_params("arbitrary"),
        name="moe_combine",
    )(slots[0], slots[1], x2, gate, meta, ys)


def _moe(x2, norm_gain, scale, shift, gate, w_router, w_gate_up, w_down, layer, li, seq):
    t, d = x2.shape
    e = w_router.shape[-1]
    blk = MOE_BLOCK
    h, meta, counts = _router(x2, norm_gain, scale, shift, jnp.transpose(w_router[li]), layer, seq)

    cnt = counts[0, :e].astype(jnp.int32)
    nblk = (cnt + blk - 1) // blk
    blk_end = jnp.cumsum(nblk)
    off = (blk_end - nblk) * blk
    n_blocks = pl.cdiv(TOP_K * t, blk) + e
    n_live = blk_end[-1:].astype(jnp.int32)
    block_ids = jnp.arange(n_blocks, dtype=jnp.int32)
    blk_expert = jnp.minimum(jnp.sum(block_ids[:, None] >= blk_end[None, :], axis=1), e - 1).astype(jnp.int32)
    ids = meta[:, 0:2].astype(jnp.int32)
    ranks = meta[:, 2:4].astype(jnp.int32)
    slots = off[ids] + ranks
    slot1, slot2 = slots[:, 0], slots[:, 1]

    xs = _dispatch(h, slot1, slot2, n_blocks * blk)
    a = _moe_up(xs, w_gate_up, li, blk_expert, n_live)
    ys = _moe_down(a, w_down, li, blk_expert, n_live)
    return _combine(ys, slot1, slot2, meta, x2, gate, seq)


def kernel(x, c, ada_w, ada_b, norm_mix, w_in, q_norm, k_norm, attn_sinks, conv_w, w_attn_branch,
           w_conv_branch, w_out, norm_ffn, ffn_w_gate_up, ffn_w_down, moe_w_router, moe_w_gate_up,
           moe_w_down):
    batch, seq, d = x.shape
    depth = ada_w.shape[0]
    d_attn = w_attn_branch.shape[1]
    d_conv = w_conv_branch.shape[1]
    n_heads = d_attn // HEAD_DIM
    d_kv = (n_heads // GQA_GROUP) * HEAD_DIM
    col_b = d_attn + 2 * d_kv
    col_ga = col_b + 3 * d_conv
    col_gc = col_ga + d
    assert w_in.shape[-1] == col_gc + d and seq % ROW_BLOCK == 0
    assert 2 * HEAD_DIM == LANES and GQA_GROUP == 4 and (n_heads // GQA_GROUP) % 2 == 0

    n_mod = ada_w.shape[-1] // d
    mod = _adaln(c, ada_w, ada_b).reshape(depth, batch, n_mod, 1, d)
    bias = _alibi_bias(n_heads)
    x2 = x.reshape(batch * seq, d)

    for l in range(depth):
        shift_m, scale_m, gate_m, shift_f, scale_f, gate_f = (mod[l, :, i] for i in range(n_mod))

        reps_q, reps_k = d_attn // HEAD_DIM, d_kv // HEAD_DIM
        head_gain = jnp.concatenate([jnp.tile(q_norm[l] * HEAD_DIM ** -0.5, reps_q),
                                     jnp.tile(k_norm[l], reps_k), jnp.ones((d_kv,), F32)])[None]
        head_mask = jnp.concatenate([jnp.ones((d_attn + d_kv,), F32), jnp.zeros((d_kv,), F32)])[None]

        h_mix = _modnorm(x2, norm_mix.reshape(depth, 1, d), scale_m, shift_m, l, seq)
        y = _inproj(h_mix, w_in, l, head_gain, head_mask)
        attn = _attention(y, attn_sinks[l], bias, batch, seq, d_attn, d_kv)
        merged = _branch_merge(attn, y, conv_w, w_attn_branch, w_conv_branch, l, seq, col_b, col_ga, col_gc)
        x2 = _resid_proj(merged, w_out, l, x2, gate_m, seq, 512, "mixer_out_proj")

        norm_f = norm_ffn.reshape(depth, 1, d)
        if l % 2 == 0:
            a = _ffn_up(x2, norm_f, scale_f, shift_f, ffn_w_gate_up, l, l // 2, seq)
            x2 = _resid_proj(a, ffn_w_down, l // 2, x2, gate_f, seq, 256, "ffn_down")
        else:
            x2 = _moe(x2, norm_f, scale_f, shift_f, gate_f, moe_w_router, moe_w_gate_up, moe_w_down,
                      l, l // 2, seq)
    return x2.reshape(batch, seq, d)
```

```python
import functools

import jax
import jax.numpy as jnp
from jax import lax
from jax.experimental import pallas as pl
from jax.experimental.pallas import tpu as pltpu

F32 = jnp.float32
BF16 = jnp.bfloat16

HEAD_DIM = 64
GQA_GROUP = 4
ATTN_BLOCK = 128
TOP_K = 2
EPS = 1e-6

LANES = 128
BF16_ROWS = 16
V7X_VMEM_BYTES = 64 * 1024 * 1024
VMEM_LIMIT = V7X_VMEM_BYTES - 8 * 1024 * 1024

ROW_BLOCK = 1024
ROUTE_BLOCK = 512
MOE_BLOCK = ROW_BLOCK
MOE_QUARTER = MOE_BLOCK // 4
MOE_CHUNK = 128
ZERO_ROWS = 256


def _params(*sem):
    return pltpu.CompilerParams(dimension_semantics=sem, vmem_limit_bytes=VMEM_LIMIT)


def _silu(x):
    return x * jax.nn.sigmoid(x)


def _modulated_norm(x, gain, one_plus_scale, shift):
    ms = jnp.mean(x * x, axis=-1, keepdims=True)
    y = (x * lax.rsqrt(ms + EPS)) * gain
    return y * one_plus_scale + shift


def _fill_modulated_norm(x_ref, gain_ref, scale_ref, shift_ref, h_ref, chunk=128):
    gain = gain_ref[...]
    one_plus = 1.0 + scale_ref[...]
    shift = shift_ref[...]

    def body(c, carry):
        r0 = pl.multiple_of(c * chunk, chunk)
        h = _modulated_norm(x_ref[pl.ds(r0, chunk), :], gain, one_plus, shift)
        h_ref[pl.ds(r0, chunk), :] = h.astype(h_ref.dtype)
        return carry

    lax.fori_loop(0, x_ref.shape[0] // chunk, body, 0)


def _adaln_kernel(c_ref, w_ref, b_ref, o_ref):
    s = _silu(c_ref[...]).astype(BF16)
    o_ref[...] = jnp.dot(s, w_ref[...].astype(BF16), preferred_element_type=F32) + b_ref[...]


def _adaln(c, ada_w, ada_b):
    depth, d, n = ada_w.shape
    b = c.shape[0]
    rows = 8 * pl.cdiv(b, 8)
    c_pad = jnp.pad(c, ((0, rows - b), (0, 0)))
    bn = 1024
    out = pl.pallas_call(
        _adaln_kernel,
        out_shape=jax.ShapeDtypeStruct((depth, rows, n), F32),
        grid=(depth, n // bn),
        in_specs=[
            pl.BlockSpec((rows, d), lambda l, j: (0, 0)),
            pl.BlockSpec((None, d, bn), lambda l, j: (l, 0, j)),
            pl.BlockSpec((None, 1, bn), lambda l, j: (l, 0, j)),
        ],
        out_specs=pl.BlockSpec((None, rows, bn), lambda l, j: (l, 0, j)),
        compiler_params=_params("arbitrary", "arbitrary"),
        name="adaln_mod",
    )(c_pad, ada_w, ada_b.reshape(depth, 1, n))
    return out[:, :b]


def _modnorm_kernel(x_ref, gain_ref, scale_ref, shift_ref, o_ref):
    _fill_modulated_norm(x_ref, gain_ref, scale_ref, shift_ref, o_ref)


def _modnorm(x2, norm_gain, scale, shift, layer, seq):
    t, d = x2.shape
    bm = ROUTE_BLOCK
    per_seq = seq // bm
    return pl.pallas_call(
        _modnorm_kernel,
        out_shape=jax.ShapeDtypeStruct((t, d), BF16),
        grid=(t // bm,),
        in_specs=[
            pl.BlockSpec((bm, d), lambda m: (m, 0)),
            pl.BlockSpec((None, 1, d), lambda m: (layer, 0, 0)),
            pl.BlockSpec((None, 1, d), lambda m: (m // per_seq, 0, 0)),
            pl.BlockSpec((None, 1, d), lambda m: (m // per_seq, 0, 0)),
        ],
        out_specs=pl.BlockSpec((bm, d), lambda m: (m, 0)),
        compiler_params=_params("arbitrary"),
        name="mixer_modnorm",
    )(x2, norm_gain, scale, shift)


def _inproj_kernel(h_ref, w_ref, hg_ref, hm_ref, o_ref, *, norm_tiles, gate_tile0):
    n = pl.program_id(1)
    acc = jnp.dot(h_ref[...], w_ref[...].astype(BF16), preferred_element_type=F32)

    @pl.when(n < norm_tiles)
    def _():
        bn = acc.shape[1]
        shift = HEAD_DIM.bit_length() - 1
        r = lax.shift_right_logical(lax.broadcasted_iota(jnp.int32, (bn, bn), 0), shift)
        c = lax.shift_right_logical(lax.broadcasted_iota(jnp.int32, (bn, bn), 1), shift)
        head_mean = jnp.where(r == c, 1.0 / HEAD_DIM, 0.0).astype(BF16)
        ms = jnp.dot((acc * acc).astype(BF16), head_mean, preferred_element_type=F32)
        normed = acc * lax.rsqrt(ms + EPS) * hg_ref[...]
        o_ref[...] = jnp.where(hm_ref[...] > 0.0, normed, acc).astype(o_ref.dtype)

    @pl.when(jnp.logical_and(n >= norm_tiles, n < gate_tile0))
    def _():
        o_ref[...] = acc.astype(o_ref.dtype)

    @pl.when(n >= gate_tile0)
    def _():
        o_ref[...] = jax.nn.sigmoid(acc).astype(o_ref.dtype)


def _inproj(h, w_in, layer, head_gain, head_mask, col_gate):
    t, d = h.shape
    n = w_in.shape[-1]
    bm, bn = 2 * ROW_BLOCK, 512
    norm_cols = head_gain.shape[-1]
    assert col_gate % bn == 0
    kern = functools.partial(_inproj_kernel, norm_tiles=norm_cols // bn, gate_tile0=col_gate // bn)
    return pl.pallas_call(
        kern,
        out_shape=jax.ShapeDtypeStruct((t, n), BF16),
        grid=(t // bm, n // bn),
        in_specs=[
            pl.BlockSpec((bm, d), lambda m, j: (m, 0)),
            pl.BlockSpec((None, d, bn), lambda m, j: (layer, 0, j)),
            pl.BlockSpec((1, bn), lambda m, j: (0, jnp.minimum(j, norm_cols // bn - 1))),
            pl.BlockSpec((1, bn), lambda m, j: (0, jnp.minimum(j, norm_cols // bn - 1))),
        ],
        out_specs=pl.BlockSpec((bm, bn), lambda m, j: (m, j)),
        compiler_params=_params("arbitrary", "arbitrary"),
        name="mixer_in_proj",
    )(h, w_in, head_gain, head_mask)


def _attn_kernel(sinks_ref, q_ref, kvp_ref, kvc_ref, bias_ref, o_ref, *, n_kv):
    blk = ATTN_BLOCK
    first = pl.program_id(1) == 0
    dkv = n_kv * HEAD_DIM
    q = q_ref[...].astype(F32)
    kvp = kvp_ref[...].astype(F32)
    kvc = kvc_ref[...].astype(F32)
    k_all = jnp.concatenate([kvp[:, :dkv], kvc[:, :dkv]], axis=0)
    v_all = jnp.concatenate([kvp[:, dkv:], kvc[:, dkv:]], axis=0)
    lo = lax.broadcasted_iota(jnp.int32, (1, LANES), 1) < HEAD_DIM
    col = lax.broadcasted_iota(jnp.int32, (1, 2 * blk), 1)
    no_prev = jnp.where(jnp.logical_and(first, col < blk), -jnp.inf, 0.0).astype(F32)

    for j in range(n_kv):
        t, half = divmod(j, 2)
        kp = k_all[:, LANES * t:LANES * (t + 1)]
        vp = v_all[:, LANES * t:LANES * (t + 1)]
        kr = pltpu.roll(kp, HEAD_DIM, 1)
        vr = pltpu.roll(vp, HEAD_DIM, 1)
        k_lo = jnp.where(lo, kr if half else kp, 0.0).astype(BF16)
        v_lo = jnp.where(lo, vr if half else vp, 0.0)
        v_hi = jnp.where(lo, 0.0, vp if half else vr)
        v_stack = jnp.concatenate([v_lo, v_hi], axis=0).astype(BF16)
        base = GQA_GROUP * HEAD_DIM * j
        qa = q[:, base:base + LANES]
        qb = q[:, base + LANES:base + 2 * LANES]
        q_stack = jnp.concatenate(
            [qa, pltpu.roll(qa, HEAD_DIM, 1), qb, pltpu.roll(qb, HEAD_DIM, 1)], axis=0).astype(BF16)
        s = lax.dot_general(q_stack, k_lo, (((1,), (1,)), ((), ())),
                            preferred_element_type=F32)
        probs, inv = [], []
        for g in range(GQA_GROUP):
            h = GQA_GROUP * j + g
            sg = s[blk * g:blk * (g + 1)] + (bias_ref[h] + no_prev)
            sink = sinks_ref[h]
            m = jnp.maximum(jnp.max(sg, axis=-1, keepdims=True), sink)
            p = jnp.exp(sg - m)
            denom = jnp.sum(p, axis=-1, keepdims=True) + jnp.exp(sink - m)
            probs.append(p.astype(BF16))
            inv.append(1.0 / denom)
        for pair in range(GQA_GROUP // 2):
            pp = jnp.concatenate([probs[2 * pair], probs[2 * pair + 1]], axis=1)
            o = jnp.dot(pp, v_stack, preferred_element_type=F32)
            o = o * jnp.where(lo, inv[2 * pair], inv[2 * pair + 1])
            o_ref[:, base + LANES * pair:base + LANES * (pair + 1)] = o.astype(o_ref.dtype)


def _alibi_bias(n_heads):
    blk = ATTN_BLOCK
    slopes = 2.0 ** (-8.0 * jnp.arange(1, n_heads + 1, dtype=F32) / n_heads)
    dist = (jnp.arange(blk) + blk)[:, None] - jnp.arange(2 * blk)[None, :]
    valid = (dist >= 0) & (dist < blk)
    bias = -(slopes[:, None, None] * dist.astype(F32)[None])
    return jnp.where(valid[None], bias, -jnp.inf)


def _attention(y, sinks, bias, batch, seq, d_attn, d_kv):
    t = y.shape[0]
    blk = ATTN_BLOCK
    nb = seq // blk
    kv_w = 2 * d_kv
    kv_col = d_attn // kv_w
    kern = functools.partial(_attn_kernel, n_kv=d_kv // HEAD_DIM)
    grid_spec = pltpu.PrefetchScalarGridSpec(
        num_scalar_prefetch=1,
        grid=(batch, nb),
        in_specs=[
            pl.BlockSpec((blk, d_attn), lambda b, i, s: (b * nb + i, 0)),
            pl.BlockSpec((blk, kv_w), lambda b, i, s: (b * nb + jnp.maximum(i - 1, 0), kv_col)),
            pl.BlockSpec((blk, kv_w), lambda b, i, s: (b * nb + i, kv_col)),
            pl.BlockSpec(bias.shape, lambda b, i, s: (0, 0, 0)),
        ],
        out_specs=pl.BlockSpec((blk, d_attn), lambda b, i, s: (b * nb + i, 0)),
    )
    return pl.pallas_call(
        kern,
        out_shape=jax.ShapeDtypeStruct((t, d_attn), BF16),
        grid_spec=grid_spec,
        compiler_params=_params("arbitrary", "arbitrary"),
        name="swa_attention",
    )(sinks, y, y, y, bias)


HALO = BF16_ROWS


def _branch_kernel(attn_ref, bc_ref, cx_ref, hbc_ref, hcx_ref, cw_ref, wa_ref, wc_ref, ga_ref, gc_ref,
                   o_ref, u_scr, conv_scr, *, blocks_per_seq, taps):
    m = pl.program_id(0)
    n = pl.program_id(1)
    dc = conv_scr.shape[1]
    half = dc // 2
    chunk = 64

    @pl.when(n == 0)
    def _():
        keep = jnp.where(m % blocks_per_seq == 0, 0.0, 1.0).astype(F32)
        hc = jnp.concatenate([hbc_ref[:, dc:], hcx_ref[:, :half]], axis=1).astype(F32)
        hx = hcx_ref[:, half:].astype(F32)
        u_scr[0:HALO, :] = hc * hx * keep

        def fill_u(c, carry):
            r0 = pl.multiple_of(c * chunk, chunk)
            cc = jnp.concatenate([bc_ref[pl.ds(r0, chunk), dc:], cx_ref[pl.ds(r0, chunk), :half]], axis=1)
            xx = cx_ref[pl.ds(r0, chunk), half:]
            u_scr[pl.ds(HALO + r0, chunk), :] = cc.astype(F32) * xx.astype(F32)
            return carry

        lax.fori_loop(0, conv_scr.shape[0] // chunk, fill_u, 0)

        def conv_rows(c, carry):
            r0 = pl.multiple_of(c * chunk, chunk)
            w0 = pl.multiple_of(r0 + HALO - 8, 8)
            win = u_scr[pl.ds(w0, chunk + 8), :]
            acc = cw_ref[taps - 1:taps, :] * win[8:]
            for back in range(1, taps):
                acc = acc + cw_ref[taps - 1 - back:taps - back, :] * pltpu.roll(win, back, 0)[8:]
            b = bc_ref[pl.ds(r0, chunk), :dc].astype(F32)
            conv_scr[pl.ds(r0, chunk), :] = (b * acc).astype(conv_scr.dtype)
            return carry

        lax.fori_loop(0, conv_scr.shape[0] // chunk, conv_rows, 0)

    a = jnp.dot(attn_ref[...], wa_ref[...].astype(BF16), preferred_element_type=F32)
    c = jnp.dot(conv_scr[...], wc_ref[...].astype(BF16), preferred_element_type=F32)
    merged = ga_ref[...].astype(F32) * a + gc_ref[...].astype(F32) * c
    o_ref[...] = merged.astype(o_ref.dtype)


def _branch_merge(attn, y, conv_w, w_attn, w_conv, layer, seq, col_b, col_ga, col_gc):
    t, d_attn = attn.shape
    dc = w_conv.shape[1]
    d = w_attn.shape[-1]
    taps = conv_w.shape[1]
    bm, bn = ROW_BLOCK, 512
    wide = dc + dc // 2
    assert col_b % wide == 0 and taps - 1 <= 8
    cb = col_b // wide
    halo_blocks = bm // HALO
    kern = functools.partial(_branch_kernel, blocks_per_seq=seq // bm, taps=taps)

    def halo_map(col):
        return lambda m, j: (jnp.maximum(m * halo_blocks - 1, 0), col)

    return pl.pallas_call(
        kern,
        out_shape=jax.ShapeDtypeStruct((t, d), BF16),
        grid=(t // bm, d // bn),
        in_specs=[
            pl.BlockSpec((bm, d_attn), lambda m, j: (m, 0)),
            pl.BlockSpec((bm, wide), lambda m, j: (m, cb)),
            pl.BlockSpec((bm, wide), lambda m, j: (m, cb + 1)),
            pl.BlockSpec((HALO, wide), halo_map(cb)),
            pl.BlockSpec((HALO, wide), halo_map(cb + 1)),
            pl.BlockSpec((None, taps, dc), lambda m, j: (layer, 0, 0)),
            pl.BlockSpec((None, d_attn, bn), lambda m, j: (layer, 0, j)),
            pl.BlockSpec((None, dc, bn), lambda m, j: (layer, 0, j)),
            pl.BlockSpec((bm, bn), lambda m, j: (m, col_ga // bn + j)),
            pl.BlockSpec((bm, bn), lambda m, j: (m, col_gc // bn + j)),
        ],
        out_specs=pl.BlockSpec((bm, bn), lambda m, j: (m, j)),
        scratch_shapes=[pltpu.VMEM((bm + HALO, dc), F32), pltpu.VMEM((bm, dc), BF16)],
        compiler_params=_params("arbitrary", "arbitrary"),
        name="branch_merge",
    )(attn, y, y, y, y, conv_w, w_attn, w_conv, y, y)


def _resid_proj_kernel(lhs_ref, w_ref, x_ref, gate_ref, o_ref):
    acc = jnp.dot(lhs_ref[...], w_ref[...].astype(BF16), preferred_element_type=F32)
    o_ref[...] = x_ref[...] + gate_ref[...] * acc


def _resid_proj(lhs, w, layer, x2, gate, seq, bm, bn, name):
    t, k = lhs.shape
    d = w.shape[-1]
    per_seq = seq // bm
    return pl.pallas_call(
        _resid_proj_kernel,
        out_shape=jax.ShapeDtypeStruct((t, d), F32),
        grid=(t // bm, d // bn),
        in_specs=[
            pl.BlockSpec((bm, k), lambda m, j: (m, 0)),
            pl.BlockSpec((None, k, bn), lambda m, j: (layer, 0, j)),
            pl.BlockSpec((bm, bn), lambda m, j: (m, j)),
            pl.BlockSpec((None, 1, bn), lambda m, j: (m // per_seq, 0, j)),
        ],
        out_specs=pl.BlockSpec((bm, bn), lambda m, j: (m, j)),
        compiler_params=_params("arbitrary", "arbitrary"),
        name=name,
    )(lhs, w, x2, gate)


def _ffn_up_kernel(x_ref, gain_ref, scale_ref, shift_ref, wg_ref, wu_ref, o_ref, h_scr):
    @pl.when(pl.program_id(1) == 0)
    def _():
        _fill_modulated_norm(x_ref, gain_ref, scale_ref, shift_ref, h_scr)

    h = h_scr[...]
    g = jnp.dot(h, wg_ref[...].astype(BF16), preferred_element_type=F32)
    u = jnp.dot(h, wu_ref[...].astype(BF16), preferred_element_type=F32)
    o_ref[...] = (_silu(g) * u).astype(o_ref.dtype)


def _ffn_up(x2, norm_gain, scale, shift, w_gate_up, layer, li, seq):
    t, d = x2.shape
    f = w_gate_up.shape[-1] // 2
    bm, bn = ROW_BLOCK, 512
    per_seq = seq // bm
    return pl.pallas_call(
        _ffn_up_kernel,
        out_shape=jax.ShapeDtypeStruct((t, f), BF16),
        grid=(t // bm, f // bn),
        in_specs=[
            pl.BlockSpec((bm, d), lambda m, j: (m, 0)),
            pl.BlockSpec((None, 1, d), lambda m, j: (layer, 0, 0)),
            pl.BlockSpec((None, 1, d), lambda m, j: (m // per_seq, 0, 0)),
            pl.BlockSpec((None, 1, d), lambda m, j: (m // per_seq, 0, 0)),
            pl.BlockSpec((None, d, bn), lambda m, j: (li, 0, j)),
            pl.BlockSpec((None, d, bn), lambda m, j: (li, 0, f // bn + j)),
        ],
        out_specs=pl.BlockSpec((bm, bn), lambda m, j: (m, j)),
        scratch_shapes=[pltpu.VMEM((bm, d), BF16)],
        compiler_params=_params("arbitrary", "arbitrary"),
        name="ffn_gate_up",
    )(x2, norm_gain, scale, shift, w_gate_up, w_gate_up)


def _router_kernel(x_ref, gain_ref, scale_ref, shift_ref, rhi_ref, rlo_ref, h_ref, meta_ref, cnt_ref,
                   tri_scr, base_scr, hi_scr, lo_scr, *, n_experts):
    i = pl.program_id(0)
    rows = x_ref.shape[0]
    chunk = 64

    @pl.when(i == 0)
    def _():
        r = lax.broadcasted_iota(jnp.int32, (rows, rows), 0)
        c = lax.broadcasted_iota(jnp.int32, (rows, rows), 1)
        tri_scr[...] = jnp.where(c < r, 1.0, 0.0).astype(BF16)
        base_scr[...] = jnp.zeros(base_scr.shape, F32)

    gain = gain_ref[...]
    one_plus = 1.0 + scale_ref[...]
    shift = shift_ref[...]

    def body(c, carry):
        r0 = pl.multiple_of(c * chunk, chunk)
        h = _modulated_norm(x_ref[pl.ds(r0, chunk), :], gain, one_plus, shift)
        h_ref[pl.ds(r0, chunk), :] = h
        hi = h.astype(BF16)
        hi_scr[pl.ds(r0, chunk), :] = hi
        lo_scr[pl.ds(r0, chunk), :] = (h - hi.astype(F32)).astype(BF16)
        return carry

    lax.fori_loop(0, rows // chunk, body, 0)

    hi = hi_scr[...]
    lg = (jnp.dot(hi, rhi_ref[...], preferred_element_type=F32)
          + jnp.dot(lo_scr[...], rhi_ref[...], preferred_element_type=F32)
          + jnp.dot(hi, rlo_ref[...], preferred_element_type=F32))
    lane_i = lax.broadcasted_iota(jnp.int32, lg.shape, 1)
    lg = jnp.where(lane_i < n_experts, lg, -jnp.inf)
    lane = lane_i.astype(F32)
    m1 = jnp.max(lg, axis=-1, keepdims=True)
    i1 = jnp.min(jnp.where(lg == m1, lane, float(LANES)), axis=-1, keepdims=True)
    lg2 = jnp.where(lane == i1, -jnp.inf, lg)
    m2 = jnp.max(lg2, axis=-1, keepdims=True)
    i2 = jnp.min(jnp.where(lg2 == m2, lane, float(LANES)), axis=-1, keepdims=True)
    ex = jnp.exp(m2 - m1)
    w1 = 1.0 / (1.0 + ex)
    w2 = ex / (1.0 + ex)
    onehot = jnp.where(jnp.logical_or(lane == i1, lane == i2), 1.0, 0.0)
    ranks = jnp.dot(tri_scr[...], onehot.astype(BF16), preferred_element_type=F32) + base_scr[...]
    r1 = jnp.sum(jnp.where(lane == i1, ranks, 0.0), axis=-1, keepdims=True)
    r2 = jnp.sum(jnp.where(lane == i2, ranks, 0.0), axis=-1, keepdims=True)
    meta = jnp.zeros(lg.shape, F32)
    for pos, val in enumerate((i1, i2, r1, r2, w1, w2)):
        meta = jnp.where(lane == float(pos), val, meta)
    meta_ref[...] = meta
    new_base = base_scr[...] + jnp.sum(onehot, axis=0, keepdims=True)
    base_scr[...] = new_base
    cnt_ref[...] = new_base


def _router(x2, norm_gain, scale, shift, w_router, layer, seq):
    t, d = x2.shape
    e = w_router.shape[-1]
    bm = ROUTE_BLOCK
    per_seq = seq // bm
    kern = functools.partial(_router_kernel, n_experts=e)
    w_pad = jnp.pad(w_router, ((0, 0), (0, LANES - e)))
    w_hi = w_pad.astype(BF16)
    w_lo = (w_pad - w_hi.astype(F32)).astype(BF16)
    w_spec = pl.BlockSpec((d, LANES), lambda m: (0, 0))
    return pl.pallas_call(
        kern,
        out_shape=(jax.ShapeDtypeStruct((t, d), F32),
                   jax.ShapeDtypeStruct((t, LANES), F32),
                   jax.ShapeDtypeStruct((1, LANES), F32)),
        grid=(t // bm,),
        in_specs=[
            pl.BlockSpec((bm, d), lambda m: (m, 0)),
            pl.BlockSpec((None, 1, d), lambda m: (layer, 0, 0)),
            pl.BlockSpec((None, 1, d), lambda m: (m // per_seq, 0, 0)),
            pl.BlockSpec((None, 1, d), lambda m: (m // per_seq, 0, 0)),
            w_spec, w_spec,
        ],
        out_specs=(pl.BlockSpec((bm, d), lambda m: (m, 0)),
                   pl.BlockSpec((bm, LANES), lambda m: (m, 0)),
                   pl.BlockSpec((1, LANES), lambda m: (0, 0))),
        scratch_shapes=[pltpu.VMEM((bm, bm), BF16), pltpu.VMEM((1, LANES), F32),
                        pltpu.VMEM((bm, d), BF16), pltpu.VMEM((bm, d), BF16)],
        compiler_params=_params("arbitrary"),
        name="moe_router",
    )(x2, norm_gain, scale, shift, w_hi, w_lo)


def _dispatch_kernel(zend_ref, znum_ref, s1_ref, s2_ref, h_ref, xs_ref, zero_scr, sem, zsem):
    rows = h_ref.shape[0]

    @pl.when(pl.program_id(0) == 0)
    def _():
        zero_scr[...] = jnp.zeros(zero_scr.shape, zero_scr.dtype)

        for e in range(zend_ref.shape[0]):
            def fill(k, e=e):
                start = pl.multiple_of(zend_ref[e] - (k + 1) * ZERO_ROWS, ZERO_ROWS)
                return pltpu.make_async_copy(zero_scr, xs_ref.at[pl.ds(start, ZERO_ROWS)], zsem)

            def start_fill(k, carry, fill=fill):
                fill(k).start()
                return carry

            def wait_fill(k, carry, fill=fill):
                fill(k).wait()
                return carry

            lax.fori_loop(0, znum_ref[e], start_fill, 0)
            lax.fori_loop(0, znum_ref[e], wait_fill, 0)

    def row_copy(i, slot_ref):
        return pltpu.make_async_copy(h_ref.at[pl.ds(i, 1)], xs_ref.at[pl.ds(slot_ref[0, i], 1)], sem)

    def body(i, carry):
        row_copy(i, s1_ref).start(priority=0)
        row_copy(i, s2_ref).start(priority=1)
        return carry

    lax.fori_loop(0, rows, body, 0, unroll=8)
    for _ in range(TOP_K):
        pltpu.make_async_copy(h_ref, xs_ref.at[pl.ds(0, rows)], sem).wait()


def _dispatch(h, slot1, slot2, zero_end, zero_chunks, n_slots):
    t, d = h.shape
    bm = ROW_BLOCK
    nblk = t // bm
    slots = [s.reshape(nblk, 1, bm) for s in (slot1, slot2)]
    smem_spec = pl.BlockSpec((None, 1, bm), lambda m, zs, zn: (m, 0, 0), memory_space=pltpu.SMEM)
    grid_spec = pltpu.PrefetchScalarGridSpec(
        num_scalar_prefetch=2,
        grid=(nblk,),
        in_specs=[smem_spec, smem_spec, pl.BlockSpec((bm, d), lambda m, zs, zn: (m, 0))],
        out_specs=pl.BlockSpec(memory_space=pl.ANY),
        scratch_shapes=[pltpu.VMEM((ZERO_ROWS, d), F32), pltpu.SemaphoreType.DMA(()),
                        pltpu.SemaphoreType.DMA(())],
    )
    return pl.pallas_call(
        _dispatch_kernel,
        out_shape=jax.ShapeDtypeStruct((n_slots, d), F32),
        grid_spec=grid_spec,
        compiler_params=_params("arbitrary"),
        name="moe_dispatch",
    )(zero_end, zero_chunks, slots[0], slots[1], h)


def _filled_quarters(rows_ref):
    shift = MOE_QUARTER.bit_length() - 1
    return lax.shift_right_logical(rows_ref[pl.program_id(0)] + (MOE_QUARTER - 1), shift)


def _moe_up_kernel(be_ref, nl_ref, rows_ref, xs_ref, wg_ref, wu_ref, o_ref, h_scr):
    del be_ref, nl_ref
    quarters = _filled_quarters(rows_ref)
    chunk = MOE_CHUNK

    @pl.when(jnp.logical_and(quarters > 0, pl.program_id(1) == 0))
    def _():
        def body(c, carry):
            r0 = pl.multiple_of(c * chunk, chunk)
            h_scr[pl.ds(r0, chunk), :] = xs_ref[pl.ds(r0, chunk), :].astype(h_scr.dtype)
            return carry

        lax.fori_loop(0, quarters * (MOE_QUARTER // chunk), body, 0)

    for q in range(MOE_BLOCK // MOE_QUARTER + 1):
        rows = q * MOE_QUARTER

        @pl.when(quarters == q)
        def _(rows=rows):
            if rows:
                h = h_scr[0:rows, :]
                g = jnp.dot(h, wg_ref[...].astype(BF16), preferred_element_type=F32)
                u = jnp.dot(h, wu_ref[...].astype(BF16), preferred_element_type=F32)
                o_ref[0:rows, :] = (_silu(g) * u).astype(o_ref.dtype)
            if rows < MOE_BLOCK:
                o_ref[rows:, :] = jnp.zeros((MOE_BLOCK - rows, o_ref.shape[1]), o_ref.dtype)


def _moe_up(xs, w_gate_up, li, blk_expert, n_live, blk_rows):
    d = xs.shape[1]
    n_blocks = blk_expert.shape[0]
    f = w_gate_up.shape[-1] // 2
    bm, bn = MOE_BLOCK, 512
    nt = f // bn

    def row(b, nl):
        return jnp.minimum(b, nl[0] - 1)

    def col(b, j, nl):
        return jnp.where(b < nl[0], j, nt - 1)

    grid_spec = pltpu.PrefetchScalarGridSpec(
        num_scalar_prefetch=3,
        grid=(n_blocks, nt),
        in_specs=[
            pl.BlockSpec((bm, d), lambda b, j, be, nl, br: (row(b, nl), 0)),
            pl.BlockSpec((None, None, d, bn),
                         lambda b, j, be, nl, br: (li, be[row(b, nl)], 0, col(b, j, nl))),
            pl.BlockSpec((None, None, d, bn),
                         lambda b, j, be, nl, br: (li, be[row(b, nl)], 0, nt + col(b, j, nl))),
        ],
        out_specs=pl.BlockSpec((bm, bn), lambda b, j, be, nl, br: (b, j)),
        scratch_shapes=[pltpu.VMEM((bm, d), BF16)],
    )
    return pl.pallas_call(
        _moe_up_kernel,
        out_shape=jax.ShapeDtypeStruct((n_blocks * bm, f), BF16),
        grid_spec=grid_spec,
        compiler_params=_params("arbitrary", "arbitrary"),
        name="moe_gate_up",
    )(blk_expert, n_live, blk_rows, xs, w_gate_up, w_gate_up)


def _moe_down_kernel(be_ref, nl_ref, rows_ref, a_ref, w_ref, o_ref):
    del be_ref, nl_ref
    quarters = _filled_quarters(rows_ref)

    for q in range(MOE_BLOCK // MOE_QUARTER + 1):
        rows = q * MOE_QUARTER

        @pl.when(quarters == q)
        def _(rows=rows):
            if rows:
                o_ref[0:rows, :] = jnp.dot(a_ref[0:rows, :], w_ref[...].astype(BF16),
                                           preferred_element_type=F32)
            if rows < MOE_BLOCK:
                o_ref[rows:, :] = jnp.zeros((MOE_BLOCK - rows, o_ref.shape[1]), o_ref.dtype)


def _moe_down(a, w_down, li, blk_expert, n_live, blk_rows):
    s, f = a.shape
    d = w_down.shape[-1]
    bm, bn = MOE_BLOCK, 256
    nt = d // bn

    def row(b, nl):
        return jnp.minimum(b, nl[0] - 1)

    def col(b, j, nl):
        return jnp.where(b < nl[0], j, nt - 1)

    grid_spec = pltpu.PrefetchScalarGridSpec(
        num_scalar_prefetch=3,
        grid=(s // bm, nt),
        in_specs=[
            pl.BlockSpec((bm, f), lambda b, j, be, nl, br: (row(b, nl), 0)),
            pl.BlockSpec((None, None, f, bn),
                         lambda b, j, be, nl, br: (li, be[row(b, nl)], 0, col(b, j, nl))),
        ],
        out_specs=pl.BlockSpec((bm, bn), lambda b, j, be, nl, br: (b, j)),
    )
    return pl.pallas_call(
        _moe_down_kernel,
        out_shape=jax.ShapeDtypeStruct((s, d), F32),
        grid_spec=grid_spec,
        compiler_params=_params("arbitrary", "arbitrary"),
        name="moe_down",
    )(blk_expert, n_live, blk_rows, a, w_down)


def _combine_kernel(s1_ref, s2_ref, x_ref, gate_ref, meta_ref, ys_ref, o_ref, g1, g2, sem):
    rows = x_ref.shape[0]
    chunk = 128

    def row_copy(i, slot_ref, dst, which):
        return pltpu.make_async_copy(ys_ref.at[pl.ds(slot_ref[0, i], 1)], dst.at[pl.ds(i, 1)],
                                     sem.at[which])

    def body(i, carry):
        row_copy(i, s1_ref, g1, 0).start(priority=0)
        row_copy(i, s2_ref, g2, 1).start(priority=1)
        return carry

    lax.fori_loop(0, rows, body, 0, unroll=8)
    pltpu.make_async_copy(ys_ref.at[pl.ds(0, rows)], g1, sem.at[0]).wait()
    pltpu.make_async_copy(ys_ref.at[pl.ds(0, rows)], g2, sem.at[1]).wait()

    gate = gate_ref[...]

    def mix(c, carry):
        r0 = pl.multiple_of(c * chunk, chunk)
        sl = pl.ds(r0, chunk)
        w1 = meta_ref[sl, 4:5]
        w2 = meta_ref[sl, 5:6]
        o_ref[sl, :] = x_ref[sl, :] + gate * (w1 * g1[sl, :] + w2 * g2[sl, :])
        return carry

    lax.fori_loop(0, rows // chunk, mix, 0)


def _combine(ys, slot1, slot2, meta, x2, gate, seq):
    t, d = x2.shape
    bm = ROUTE_BLOCK
    nblk = t // bm
    per_seq = seq // bm
    slots = [s.reshape(nblk, 1, bm) for s in (slot1, slot2)]
    smem_spec = pl.BlockSpec((None, 1, bm), lambda m: (m, 0, 0), memory_space=pltpu.SMEM)
    return pl.pallas_call(
        _combine_kernel,
        out_shape=jax.ShapeDtypeStruct((t, d), F32),
        grid=(nblk,),
        in_specs=[smem_spec, smem_spec,
                  pl.BlockSpec((bm, d), lambda m: (m, 0)),
                  pl.BlockSpec((None, 1, d), lambda m: (m // per_seq, 0, 0)),
                  pl.BlockSpec((bm, LANES), lambda m: (m, 0)),
                  pl.BlockSpec(memory_space=pl.ANY)],
        out_specs=pl.BlockSpec((bm, d), lambda m: (m, 0)),
        scratch_shapes=[pltpu.VMEM((bm, d), F32), pltpu.VMEM((bm, d), F32),
                        pltpu.SemaphoreType.DMA((2,))],
        compiler_params=_params("arbitrary"),
        name="moe_combine",
    )(slots[0], slots[1], x2, gate, meta, ys)


def _moe(x2, norm_gain, scale, shift, gate, w_router, w_gate_up, w_down, layer, li, seq):
    t, d = x2.shape
    e = w_router.shape[-1]
    blk = MOE_BLOCK
    h, meta, counts = _router(x2, norm_gain, scale, shift, w_router[li], layer, seq)

    cnt = counts[0, :e].astype(jnp.int32)
    nblk = (cnt + blk - 1) // blk
    blk_end = jnp.cumsum(nblk)
    blk_start = blk_end - nblk
    off = blk_start * blk
    n_blocks = pl.cdiv(TOP_K * t, blk) + e
    n_live = blk_end[-1:].astype(jnp.int32)
    block_ids = jnp.arange(n_blocks, dtype=jnp.int32)
    blk_expert = jnp.minimum(jnp.sum(block_ids[:, None] >= blk_end[None, :], axis=1), e - 1).astype(jnp.int32)
    blk_rows = jnp.clip(cnt[blk_expert] - (block_ids - blk_start[blk_expert]) * blk, 0, blk)
    blk_rows = jnp.where(block_ids < n_live[0], blk_rows, 0).astype(jnp.int32)
    ids = meta[:, 0:2].astype(jnp.int32)
    ranks = meta[:, 2:4].astype(jnp.int32)
    slots = off[ids] + ranks
    slot1, slot2 = slots[:, 0], slots[:, 1]

    n_slots = n_blocks * blk
    zero_end = jnp.concatenate([off[1:], jnp.full((1,), n_slots, jnp.int32)]).astype(jnp.int32)
    zero_chunks = ((zero_end - (off + cnt) + ZERO_ROWS - 1) // ZERO_ROWS).astype(jnp.int32)
    xs = _dispatch(h, slot1, slot2, zero_end, zero_chunks, n_slots)
    a = _moe_up(xs, w_gate_up, li, blk_expert, n_live, blk_rows)
    ys = _moe_down(a, w_down, li, blk_expert, n_live, blk_rows)
    return _combine(ys, slot1, slot2, meta, x2, gate, seq)


def kernel(x, c, ada_w, ada_b, norm_mix, w_in, q_norm, k_norm, attn_sinks, conv_w, w_attn_branch,
           w_conv_branch, w_out, norm_ffn, ffn_w_gate_up, ffn_w_down, moe_w_router, moe_w_gate_up,
           moe_w_down):
    batch, seq, d = x.shape
    depth = ada_w.shape[0]
    d_attn = w_attn_branch.shape[1]
    d_conv = w_conv_branch.shape[1]
    n_heads = d_attn // HEAD_DIM
    d_kv = (n_heads // GQA_GROUP) * HEAD_DIM
    col_b = d_attn + 2 * d_kv
    col_ga = col_b + 3 * d_conv
    col_gc = col_ga + d
    assert w_in.shape[-1] == col_gc + d and seq % ROW_BLOCK == 0
    assert 2 * HEAD_DIM == LANES and GQA_GROUP == 4 and (n_heads // GQA_GROUP) % 2 == 0

    n_mod = ada_w.shape[-1] // d
    mod = _adaln(c, ada_w, ada_b).reshape(depth, batch, n_mod, 1, d)
    bias = _alibi_bias(n_heads)
    x2 = x.reshape(batch * seq, d)

    for l in range(depth):
        shift_m, scale_m, gate_m, shift_f, scale_f, gate_f = (mod[l, :, i] for i in range(n_mod))

        reps_q, reps_k = d_attn // HEAD_DIM, d_kv // HEAD_DIM
        head_gain = jnp.concatenate([jnp.tile(q_norm[l] * HEAD_DIM ** -0.5, reps_q),
                                     jnp.tile(k_norm[l], reps_k), jnp.ones((d_kv,), F32)])[None]
        head_mask = jnp.concatenate([jnp.ones((d_attn + d_kv,), F32), jnp.zeros((d_kv,), F32)])[None]

        h_mix = _modnorm(x2, norm_mix.reshape(depth, 1, d), scale_m, shift_m, l, seq)
        y = _inproj(h_mix, w_in, l, head_gain, head_mask, col_ga)
        attn = _attention(y, attn_sinks[l], bias, batch, seq, d_attn, d_kv)
        merged = _branch_merge(attn, y, conv_w, w_attn_branch, w_conv_branch, l, seq, col_b, col_ga, col_gc)
        x2 = _resid_proj(merged, w_out, l, x2, gate_m, seq, 2 * ROW_BLOCK, 512, "mixer_out_proj")

        norm_f = norm_ffn.reshape(depth, 1, d)
        if l % 2 == 0:
            a = _ffn_up(x2, norm_f, scale_f, shift_f, ffn_w_gate_up, l, l // 2, seq)
            x2 = _resid_proj(a, ffn_w_down, l // 2, x2, gate_f, seq, ROW_BLOCK, 256, "ffn_down")
        else:
            x2 = _moe(x2, norm_f, scale_f, shift_f, gate_f, moe_w_router, moe_w_gate_up, moe_w_down,
                      l, l // 2, seq)
    return x2.reshape(batch, seq, d)
```

```python
import functools

import jax
import jax.numpy as jnp
from jax import lax
from jax.experimental import pallas as pl
from jax.experimental.pallas import tpu as pltpu

F32 = jnp.float32
BF16 = jnp.bfloat16

HEAD_DIM = 64
GQA_GROUP = 4
ATTN_BLOCK = 128
TOP_K = 2
EPS = 1e-6

LANES = 128
BF16_ROWS = 16
V7X_VMEM_BYTES = 64 * 1024 * 1024
VMEM_LIMIT = V7X_VMEM_BYTES - 8 * 1024 * 1024

ROW_BLOCK = 1024
ROUTE_BLOCK = 512
MOE_BLOCK = ROW_BLOCK
MOE_QUARTER = MOE_BLOCK // 4
MOE_CHUNK = 128
ZERO_ROWS = 256


def _params(*sem):
    return pltpu.CompilerParams(dimension_semantics=sem, vmem_limit_bytes=VMEM_LIMIT)


def _sigmoid(x):
    return 0.5 + 0.5 * jnp.tanh(0.5 * x)


def _silu(x):
    half = 0.5 * x
    return half + half * jnp.tanh(half)


MXU_COLS = 256


def _swiglu_tile(h, wg_ref, wu_ref, o_ref, rows):
    for c0 in range(0, o_ref.shape[1], MXU_COLS):
        cols = slice(c0, c0 + MXU_COLS)
        g = jnp.dot(h, wg_ref[:, cols].astype(BF16), preferred_element_type=F32)
        u = jnp.dot(h, wu_ref[:, cols].astype(BF16), preferred_element_type=F32)
        o_ref[0:rows, cols] = (_silu(g) * u).astype(o_ref.dtype)


def _modulated_norm(x, gain, one_plus_scale, shift):
    ms = jnp.mean(x * x, axis=-1, keepdims=True)
    y = (x * lax.rsqrt(ms + EPS)) * gain
    return y * one_plus_scale + shift


def _fill_modulated_norm(x_ref, gain_ref, scale_ref, shift_ref, h_ref, chunk=128):
    gain = gain_ref[...]
    one_plus = 1.0 + scale_ref[...]
    shift = shift_ref[...]

    def body(c, carry):
        r0 = pl.multiple_of(c * chunk, chunk)
        h = _modulated_norm(x_ref[pl.ds(r0, chunk), :], gain, one_plus, shift)
        h_ref[pl.ds(r0, chunk), :] = h.astype(h_ref.dtype)
        return carry

    lax.fori_loop(0, x_ref.shape[0] // chunk, body, 0)


def _adaln_kernel(c_ref, w_ref, b_ref, o_ref):
    s = _silu(c_ref[...]).astype(BF16)
    o_ref[...] = jnp.dot(s, w_ref[...].astype(BF16), preferred_element_type=F32) + b_ref[...]


def _adaln(c, ada_w, ada_b):
    depth, d, n = ada_w.shape
    b = c.shape[0]
    rows = 8 * pl.cdiv(b, 8)
    c_pad = jnp.pad(c, ((0, rows - b), (0, 0)))
    bn = 1024
    out = pl.pallas_call(
        _adaln_kernel,
        out_shape=jax.ShapeDtypeStruct((depth, rows, n), F32),
        grid=(depth, n // bn),
        in_specs=[
            pl.BlockSpec((rows, d), lambda l, j: (0, 0)),
            pl.BlockSpec((None, d, bn), lambda l, j: (l, 0, j)),
            pl.BlockSpec((None, 1, bn), lambda l, j: (l, 0, j)),
        ],
        out_specs=pl.BlockSpec((None, rows, bn), lambda l, j: (l, 0, j)),
        compiler_params=_params("arbitrary", "arbitrary"),
        name="adaln_mod",
    )(c_pad, ada_w, ada_b.reshape(depth, 1, n))
    return out[:, :b]


def _modnorm_kernel(x_ref, gain_ref, scale_ref, shift_ref, o_ref):
    _fill_modulated_norm(x_ref, gain_ref, scale_ref, shift_ref, o_ref)


def _modnorm(x2, norm_gain, scale, shift, layer, seq):
    t, d = x2.shape
    bm = ROUTE_BLOCK
    per_seq = seq // bm
    return pl.pallas_call(
        _modnorm_kernel,
        out_shape=jax.ShapeDtypeStruct((t, d), BF16),
        grid=(t // bm,),
        in_specs=[
            pl.BlockSpec((bm, d), lambda m: (m, 0)),
            pl.BlockSpec((None, 1, d), lambda m: (layer, 0, 0)),
            pl.BlockSpec((None, 1, d), lambda m: (m // per_seq, 0, 0)),
            pl.BlockSpec((None, 1, d), lambda m: (m // per_seq, 0, 0)),
        ],
        out_specs=pl.BlockSpec((bm, d), lambda m: (m, 0)),
        compiler_params=_params("arbitrary"),
        name="mixer_modnorm",
    )(x2, norm_gain, scale, shift)


def _inproj_kernel(h_ref, w_ref, hg_ref, hm_ref, o_ref, *, norm_tiles):
    n = pl.program_id(1)
    acc = jnp.dot(h_ref[...], w_ref[...].astype(BF16), preferred_element_type=F32)

    @pl.when(n < norm_tiles)
    def _():
        bn = acc.shape[1]
        shift = HEAD_DIM.bit_length() - 1
        r = lax.shift_right_logical(lax.broadcasted_iota(jnp.int32, (bn, bn), 0), shift)
        c = lax.shift_right_logical(lax.broadcasted_iota(jnp.int32, (bn, bn), 1), shift)
        head_mean = jnp.where(r == c, 1.0 / HEAD_DIM, 0.0).astype(BF16)
        ms = jnp.dot((acc * acc).astype(BF16), head_mean, preferred_element_type=F32)
        normed = acc * lax.rsqrt(ms + EPS) * hg_ref[...]
        o_ref[...] = jnp.where(hm_ref[...] > 0.0, normed, acc).astype(o_ref.dtype)

    @pl.when(n >= norm_tiles)
    def _():
        o_ref[...] = acc.astype(o_ref.dtype)


def _inproj(h, w_in, layer, head_gain, head_mask):
    t, d = h.shape
    n = w_in.shape[-1]
    bm, bn = 2 * ROW_BLOCK, 512
    norm_cols = head_gain.shape[-1]
    kern = functools.partial(_inproj_kernel, norm_tiles=norm_cols // bn)
    return pl.pallas_call(
        kern,
        out_shape=jax.ShapeDtypeStruct((t, n), BF16),
        grid=(t // bm, n // bn),
        in_specs=[
            pl.BlockSpec((bm, d), lambda m, j: (m, 0)),
            pl.BlockSpec((None, d, bn), lambda m, j: (layer, 0, j)),
            pl.BlockSpec((1, bn), lambda m, j: (0, jnp.minimum(j, norm_cols // bn - 1))),
            pl.BlockSpec((1, bn), lambda m, j: (0, jnp.minimum(j, norm_cols // bn - 1))),
        ],
        out_specs=pl.BlockSpec((bm, bn), lambda m, j: (m, j)),
        compiler_params=_params("arbitrary", "arbitrary"),
        name="mixer_in_proj",
    )(h, w_in, head_gain, head_mask)


ATTN_STEP_BLOCKS = 2


def _attn_kernel(sinks_ref, q_ref, kvp_ref, kvc_ref, bias_ref, o_ref, *, n_kv):
    blk = ATTN_BLOCK
    first = pl.program_id(1) == 0
    dkv = n_kv * HEAD_DIM
    q = q_ref[...].astype(F32)
    kv = jnp.concatenate([kvp_ref[...], kvc_ref[...]], axis=0).astype(F32)
    k_all = kv[:, :dkv]
    v_all = kv[:, dkv:]
    lo = lax.broadcasted_iota(jnp.int32, (1, LANES), 1) < HEAD_DIM
    col = lax.broadcasted_iota(jnp.int32, (1, 2 * blk), 1)
    no_prev = jnp.where(jnp.logical_and(first, col < blk), -jnp.inf, 0.0).astype(F32)
    first_head = lax.broadcasted_iota(jnp.int32, (4 * blk, LANES), 0) < 2 * blk
    lo_full = lax.broadcasted_iota(jnp.int32, (4 * blk, LANES), 1) < HEAD_DIM
    sum_cols = jnp.where(first_head == lo_full, 1.0, 0.0).astype(BF16)

    for j in range(n_kv):
        t, half = divmod(j, 2)
        kp = k_all[:, LANES * t:LANES * (t + 1)]
        vp = v_all[:, LANES * t:LANES * (t + 1)]
        kr = pltpu.roll(kp, HEAD_DIM, 1)
        vr = pltpu.roll(vp, HEAD_DIM, 1)
        k_lo = jnp.where(lo, kr if half else kp, 0.0).astype(BF16)
        v_lo = jnp.where(lo, vr if half else vp, 0.0).astype(BF16)
        v_hi = jnp.where(lo, 0.0, vp if half else vr).astype(BF16)
        base = GQA_GROUP * HEAD_DIM * j
        for sb in range(ATTN_STEP_BLOCKS):
            rows = slice(blk * sb, blk * (sb + 1))
            keys = slice(blk * sb, blk * (sb + 2))
            v_stack = jnp.concatenate(
                [jnp.concatenate([v_lo[keys], v_hi[keys]], axis=0), sum_cols], axis=1)
            qa = q[rows, base:base + LANES]
            qb = q[rows, base + LANES:base + 2 * LANES]
            q_stack = jnp.concatenate(
                [qa, pltpu.roll(qa, HEAD_DIM, 1), qb, pltpu.roll(qb, HEAD_DIM, 1)], axis=0).astype(BF16)
            s = lax.dot_general(q_stack, k_lo[keys], (((1,), (1,)), ((), ())),
                                preferred_element_type=F32)
            probs, sink_w = [], []
            for g in range(GQA_GROUP):
                h = GQA_GROUP * j + g
                bias = bias_ref[h] + no_prev if sb == 0 else bias_ref[h]
                sg = s[blk * g:blk * (g + 1)] + bias
                sink = sinks_ref[h]
                m = jnp.maximum(jnp.max(sg, axis=-1, keepdims=True), sink)
                probs.append(jnp.exp(sg - m).astype(BF16))
                sink_w.append(jnp.exp(sink - m))
            for pair in range(GQA_GROUP // 2):
                pp = jnp.concatenate([probs[2 * pair], probs[2 * pair + 1]], axis=1)
                o = jnp.dot(pp, v_stack, preferred_element_type=F32)
                denom = o[:, LANES:] + jnp.where(lo, sink_w[2 * pair], sink_w[2 * pair + 1])
                o_ref[rows, base + LANES * pair:base + LANES * (pair + 1)] = (
                    o[:, :LANES] / denom).astype(o_ref.dtype)


def _alibi_bias(n_heads):
    blk = ATTN_BLOCK
    slopes = 2.0 ** (-8.0 * jnp.arange(1, n_heads + 1, dtype=F32) / n_heads)
    dist = (jnp.arange(blk) + blk)[:, None] - jnp.arange(2 * blk)[None, :]
    valid = (dist >= 0) & (dist < blk)
    bias = -(slopes[:, None, None] * dist.astype(F32)[None])
    return jnp.where(valid[None], bias, -jnp.inf)


def _attention(y, sinks, bias, batch, seq, d_attn, d_kv):
    t = y.shape[0]
    blk = ATTN_BLOCK
    step = ATTN_STEP_BLOCKS * blk
    ns = seq // step
    kv_w = 2 * d_kv
    kv_col = d_attn // kv_w
    kern = functools.partial(_attn_kernel, n_kv=d_kv // HEAD_DIM)
    grid_spec = pltpu.PrefetchScalarGridSpec(
        num_scalar_prefetch=1,
        grid=(batch, ns),
        in_specs=[
            pl.BlockSpec((step, d_attn), lambda b, i, s: (b * ns + i, 0)),
            pl.BlockSpec((blk, kv_w),
                         lambda b, i, s: ((b * ns + i) * ATTN_STEP_BLOCKS - jnp.minimum(i, 1), kv_col)),
            pl.BlockSpec((step, kv_w), lambda b, i, s: (b * ns + i, kv_col)),
            pl.BlockSpec(bias.shape, lambda b, i, s: (0, 0, 0)),
        ],
        out_specs=pl.BlockSpec((step, d_attn), lambda b, i, s: (b * ns + i, 0)),
    )
    return pl.pallas_call(
        kern,
        out_shape=jax.ShapeDtypeStruct((t, d_attn), BF16),
        grid_spec=grid_spec,
        compiler_params=_params("arbitrary", "arbitrary"),
        name="swa_attention",
    )(sinks, y, y, y, bias)


HALO = BF16_ROWS


def _branch_kernel(attn_ref, bc_ref, cx_ref, hbc_ref, hcx_ref, cw_ref, wa_ref, wc_ref, ga_ref, gc_ref,
                   o_ref, u_scr, conv_scr, *, blocks_per_seq, taps):
    m = pl.program_id(0)
    n = pl.program_id(1)
    dc = conv_scr.shape[1]
    half = dc // 2
    chunk = 64

    @pl.when(n == 0)
    def _():
        keep = jnp.where(m % blocks_per_seq == 0, 0.0, 1.0).astype(F32)
        hc = jnp.concatenate([hbc_ref[:, dc:], hcx_ref[:, :half]], axis=1).astype(F32)
        hx = hcx_ref[:, half:].astype(F32)
        u_scr[0:HALO, :] = hc * hx * keep

        def fill_u(c, carry):
            r0 = pl.multiple_of(c * chunk, chunk)
            cc = jnp.concatenate([bc_ref[pl.ds(r0, chunk), dc:], cx_ref[pl.ds(r0, chunk), :half]], axis=1)
            xx = cx_ref[pl.ds(r0, chunk), half:]
            u_scr[pl.ds(HALO + r0, chunk), :] = cc.astype(F32) * xx.astype(F32)
            return carry

        lax.fori_loop(0, conv_scr.shape[0] // chunk, fill_u, 0)

        def conv_rows(c, carry):
            r0 = pl.multiple_of(c * chunk, chunk)
            w0 = pl.multiple_of(r0 + HALO - 8, 8)
            win = u_scr[pl.ds(w0, chunk + 8), :]
            acc = cw_ref[taps - 1:taps, :] * win[8:]
            for back in range(1, taps):
                acc = acc + cw_ref[taps - 1 - back:taps - back, :] * pltpu.roll(win, back, 0)[8:]
            b = bc_ref[pl.ds(r0, chunk), :dc].astype(F32)
            conv_scr[pl.ds(r0, chunk), :] = (b * acc).astype(conv_scr.dtype)
            return carry

        lax.fori_loop(0, conv_scr.shape[0] // chunk, conv_rows, 0)

    a = jnp.dot(attn_ref[...], wa_ref[...].astype(BF16), preferred_element_type=F32)
    c = jnp.dot(conv_scr[...], wc_ref[...].astype(BF16), preferred_element_type=F32)
    merged = _sigmoid(ga_ref[...].astype(F32)) * a + _sigmoid(gc_ref[...].astype(F32)) * c
    o_ref[...] = merged.astype(o_ref.dtype)


def _branch_merge(attn, y, conv_w, w_attn, w_conv, layer, seq, col_b, col_ga, col_gc):
    t, d_attn = attn.shape
    dc = w_conv.shape[1]
    d = w_attn.shape[-1]
    taps = conv_w.shape[1]
    bm, bn = ROW_BLOCK, 512
    wide = dc + dc // 2
    assert col_b % wide == 0 and taps - 1 <= 8
    cb = col_b // wide
    halo_blocks = bm // HALO
    kern = functools.partial(_branch_kernel, blocks_per_seq=seq // bm, taps=taps)

    def halo_map(col):
        return lambda m, j: (jnp.maximum(m * halo_blocks - 1, 0), col)

    return pl.pallas_call(
        kern,
        out_shape=jax.ShapeDtypeStruct((t, d), BF16),
        grid=(t // bm, d // bn),
        in_specs=[
            pl.BlockSpec((bm, d_attn), lambda m, j: (m, 0)),
            pl.BlockSpec((bm, wide), lambda m, j: (m, cb)),
            pl.BlockSpec((bm, wide), lambda m, j: (m, cb + 1)),
            pl.BlockSpec((HALO, wide), halo_map(cb)),
            pl.BlockSpec((HALO, wide), halo_map(cb + 1)),
            pl.BlockSpec((None, taps, dc), lambda m, j: (layer, 0, 0)),
            pl.BlockSpec((None, d_attn, bn), lambda m, j: (layer, 0, j)),
            pl.BlockSpec((None, dc, bn), lambda m, j: (layer, 0, j)),
            pl.BlockSpec((bm, bn), lambda m, j: (m, col_ga // bn + j)),
            pl.BlockSpec((bm, bn), lambda m, j: (m, col_gc // bn + j)),
        ],
        out_specs=pl.BlockSpec((bm, bn), lambda m, j: (m, j)),
        scratch_shapes=[pltpu.VMEM((bm + HALO, dc), F32), pltpu.VMEM((bm, dc), BF16)],
        compiler_params=_params("arbitrary", "arbitrary"),
        name="branch_merge",
    )(attn, y, y, y, y, conv_w, w_attn, w_conv, y, y)


def _resid_proj_kernel(lhs_ref, w_ref, x_ref, gate_ref, o_ref):
    acc = jnp.dot(lhs_ref[...], w_ref[...].astype(BF16), preferred_element_type=F32)
    o_ref[...] = x_ref[...] + gate_ref[...] * acc


def _resid_proj(lhs, w, layer, x2, gate, seq, bm, bn, name):
    t, k = lhs.shape
    d = w.shape[-1]
    per_seq = seq // bm
    return pl.pallas_call(
        _resid_proj_kernel,
        out_shape=jax.ShapeDtypeStruct((t, d), F32),
        grid=(t // bm, d // bn),
        in_specs=[
            pl.BlockSpec((bm, k), lambda m, j: (m, 0)),
            pl.BlockSpec((None, k, bn), lambda m, j: (layer, 0, j)),
            pl.BlockSpec((bm, bn), lambda m, j: (m, j)),
            pl.BlockSpec((None, 1, bn), lambda m, j: (m // per_seq, 0, j)),
        ],
        out_specs=pl.BlockSpec((bm, bn), lambda m, j: (m, j)),
        compiler_params=_params("arbitrary", "arbitrary"),
        name=name,
    )(lhs, w, x2, gate)


def _ffn_up_kernel(x_ref, gain_ref, scale_ref, shift_ref, wg_ref, wu_ref, o_ref, h_scr):
    @pl.when(pl.program_id(1) == 0)
    def _():
        _fill_modulated_norm(x_ref, gain_ref, scale_ref, shift_ref, h_scr)

    _swiglu_tile(h_scr[...], wg_ref, wu_ref, o_ref, o_ref.shape[0])


def _ffn_up(x2, norm_gain, scale, shift, w_gate_up, layer, li, seq):
    t, d = x2.shape
    f = w_gate_up.shape[-1] // 2
    bm, bn = ROW_BLOCK, 512
    per_seq = seq // bm
    return pl.pallas_call(
        _ffn_up_kernel,
        out_shape=jax.ShapeDtypeStruct((t, f), BF16),
        grid=(t // bm, f // bn),
        in_specs=[
            pl.BlockSpec((bm, d), lambda m, j: (m, 0)),
            pl.BlockSpec((None, 1, d), lambda m, j: (layer, 0, 0)),
            pl.BlockSpec((None, 1, d), lambda m, j: (m // per_seq, 0, 0)),
            pl.BlockSpec((None, 1, d), lambda m, j: (m // per_seq, 0, 0)),
            pl.BlockSpec((None, d, bn), lambda m, j: (li, 0, j)),
            pl.BlockSpec((None, d, bn), lambda m, j: (li, 0, f // bn + j)),
        ],
        out_specs=pl.BlockSpec((bm, bn), lambda m, j: (m, j)),
        scratch_shapes=[pltpu.VMEM((bm, d), BF16)],
        compiler_params=_params("arbitrary", "arbitrary"),
        name="ffn_gate_up",
    )(x2, norm_gain, scale, shift, w_gate_up, w_gate_up)


def _router_kernel(x_ref, gain_ref, scale_ref, shift_ref, rhi_ref, rlo_ref, h_ref, meta_ref, cnt_ref,
                   tri_scr, base_scr, hi_scr, lo_scr, *, n_experts):
    i = pl.program_id(0)
    rows = x_ref.shape[0]
    chunk = 64

    @pl.when(i == 0)
    def _():
        r = lax.broadcasted_iota(jnp.int32, (rows, rows), 0)
        c = lax.broadcasted_iota(jnp.int32, (rows, rows), 1)
        tri_scr[...] = jnp.where(c < r, 1.0, 0.0).astype(BF16)
        base_scr[...] = jnp.zeros(base_scr.shape, F32)

    gain = gain_ref[...]
    one_plus = 1.0 + scale_ref[...]
    shift = shift_ref[...]

    def body(c, carry):
        r0 = pl.multiple_of(c * chunk, chunk)
        h = _modulated_norm(x_ref[pl.ds(r0, chunk), :], gain, one_plus, shift)
        h_ref[pl.ds(r0, chunk), :] = h
        hi = h.astype(BF16)
        hi_scr[pl.ds(r0, chunk), :] = hi
        lo_scr[pl.ds(r0, chunk), :] = (h - hi.astype(F32)).astype(BF16)
        return carry

    lax.fori_loop(0, rows // chunk, body, 0)

    hi = hi_scr[...]
    lg = (jnp.dot(hi, rhi_ref[...], preferred_element_type=F32)
          + jnp.dot(lo_scr[...], rhi_ref[...], preferred_element_type=F32)
          + jnp.dot(hi, rlo_ref[...], preferred_element_type=F32))
    lane_i = lax.broadcasted_iota(jnp.int32, lg.shape, 1)
    lg = jnp.where(lane_i < n_experts, lg, -jnp.inf)
    lane = lane_i.astype(F32)
    m1 = jnp.max(lg, axis=-1, keepdims=True)
    i1 = jnp.min(jnp.where(lg == m1, lane, float(LANES)), axis=-1, keepdims=True)
    lg2 = jnp.where(lane == i1, -jnp.inf, lg)
    m2 = jnp.max(lg2, axis=-1, keepdims=True)
    i2 = jnp.min(jnp.where(lg2 == m2, lane, float(LANES)), axis=-1, keepdims=True)
    ex = jnp.exp(m2 - m1)
    w1 = 1.0 / (1.0 + ex)
    w2 = ex / (1.0 + ex)
    onehot = jnp.where(jnp.logical_or(lane == i1, lane == i2), 1.0, 0.0)
    ranks = jnp.dot(tri_scr[...], onehot.astype(BF16), preferred_element_type=F32) + base_scr[...]
    r1 = jnp.sum(jnp.where(lane == i1, ranks, 0.0), axis=-1, keepdims=True)
    r2 = jnp.sum(jnp.where(lane == i2, ranks, 0.0), axis=-1, keepdims=True)
    meta = jnp.zeros(lg.shape, F32)
    for pos, val in enumerate((i1, i2, r1, r2, w1, w2)):
        meta = jnp.where(lane == float(pos), val, meta)
    meta_ref[...] = meta
    new_base = base_scr[...] + jnp.sum(onehot, axis=0, keepdims=True)
    base_scr[...] = new_base
    cnt_ref[...] = new_base


def _router(x2, norm_gain, scale, shift, w_router, layer, seq):
    t, d = x2.shape
    e = w_router.shape[-1]
    bm = ROUTE_BLOCK
    per_seq = seq // bm
    kern = functools.partial(_router_kernel, n_experts=e)
    w_pad = jnp.pad(w_router, ((0, 0), (0, LANES - e)))
    w_hi = w_pad.astype(BF16)
    w_lo = (w_pad - w_hi.astype(F32)).astype(BF16)
    w_spec = pl.BlockSpec((d, LANES), lambda m: (0, 0))
    return pl.pallas_call(
        kern,
        out_shape=(jax.ShapeDtypeStruct((t, d), F32),
                   jax.ShapeDtypeStruct((t, LANES), F32),
                   jax.ShapeDtypeStruct((1, LANES), F32)),
        grid=(t // bm,),
        in_specs=[
            pl.BlockSpec((bm, d), lambda m: (m, 0)),
            pl.BlockSpec((None, 1, d), lambda m: (layer, 0, 0)),
            pl.BlockSpec((None, 1, d), lambda m: (m // per_seq, 0, 0)),
            pl.BlockSpec((None, 1, d), lambda m: (m // per_seq, 0, 0)),
            w_spec, w_spec,
        ],
        out_specs=(pl.BlockSpec((bm, d), lambda m: (m, 0)),
                   pl.BlockSpec((bm, LANES), lambda m: (m, 0)),
                   pl.BlockSpec((1, LANES), lambda m: (0, 0))),
        scratch_shapes=[pltpu.VMEM((bm, bm), BF16), pltpu.VMEM((1, LANES), F32),
                        pltpu.VMEM((bm, d), BF16), pltpu.VMEM((bm, d), BF16)],
        compiler_params=_params("arbitrary"),
        name="moe_router",
    )(x2, norm_gain, scale, shift, w_hi, w_lo)


def _dispatch_kernel(zend_ref, znum_ref, s1_ref, s2_ref, h_ref, xs_ref, zero_scr, sem, zsem):
    rows = h_ref.shape[0]

    @pl.when(pl.program_id(0) == 0)
    def _():
        zero_scr[...] = jnp.zeros(zero_scr.shape, zero_scr.dtype)

        for e in range(zend_ref.shape[0]):
            def fill(k, e=e):
                start = pl.multiple_of(zend_ref[e] - (k + 1) * ZERO_ROWS, ZERO_ROWS)
                return pltpu.make_async_copy(zero_scr, xs_ref.at[pl.ds(start, ZERO_ROWS)], zsem)

            def start_fill(k, carry, fill=fill):
                fill(k).start()
                return carry

            def wait_fill(k, carry, fill=fill):
                fill(k).wait()
                return carry

            lax.fori_loop(0, znum_ref[e], start_fill, 0)
            lax.fori_loop(0, znum_ref[e], wait_fill, 0)

    def row_copy(i, slot_ref):
        return pltpu.make_async_copy(h_ref.at[pl.ds(i, 1)], xs_ref.at[pl.ds(slot_ref[0, i], 1)], sem)

    def body(i, carry):
        row_copy(i, s1_ref).start(priority=0)
        row_copy(i, s2_ref).start(priority=1)
        return carry

    lax.fori_loop(0, rows, body, 0, unroll=8)
    for _ in range(TOP_K):
        pltpu.make_async_copy(h_ref, xs_ref.at[pl.ds(0, rows)], sem).wait()


def _dispatch(h, slot1, slot2, zero_end, zero_chunks, n_slots):
    t, d = h.shape
    bm = ROW_BLOCK
    nblk = t // bm
    slots = [s.reshape(nblk, 1, bm) for s in (slot1, slot2)]
    smem_spec = pl.BlockSpec((None, 1, bm), lambda m, zs, zn: (m, 0, 0), memory_space=pltpu.SMEM)
    grid_spec = pltpu.PrefetchScalarGridSpec(
        num_scalar_prefetch=2,
        grid=(nblk,),
        in_specs=[smem_spec, smem_spec, pl.BlockSpec((bm, d), lambda m, zs, zn: (m, 0))],
        out_specs=pl.BlockSpec(memory_space=pl.ANY),
        scratch_shapes=[pltpu.VMEM((ZERO_ROWS, d), F32), pltpu.SemaphoreType.DMA(()),
                        pltpu.SemaphoreType.DMA(())],
    )
    return pl.pallas_call(
        _dispatch_kernel,
        out_shape=jax.ShapeDtypeStruct((n_slots, d), F32),
        grid_spec=grid_spec,
        compiler_params=_params("arbitrary"),
        name="moe_dispatch",
    )(zero_end, zero_chunks, slots[0], slots[1], h)


def _filled_quarters(rows_ref):
    shift = MOE_QUARTER.bit_length() - 1
    return lax.shift_right_logical(rows_ref[pl.program_id(0)] + (MOE_QUARTER - 1), shift)


def _moe_up_kernel(be_ref, nl_ref, rows_ref, xs_ref, wg_ref, wu_ref, o_ref, h_scr):
    del be_ref, nl_ref
    quarters = _filled_quarters(rows_ref)
    chunk = MOE_CHUNK

    @pl.when(jnp.logical_and(quarters > 0, pl.program_id(1) == 0))
    def _():
        def body(c, carry):
            r0 = pl.multiple_of(c * chunk, chunk)
            h_scr[pl.ds(r0, chunk), :] = xs_ref[pl.ds(r0, chunk), :].astype(h_scr.dtype)
            return carry

        lax.fori_loop(0, quarters * (MOE_QUARTER // chunk), body, 0)

    for q in range(MOE_BLOCK // MOE_QUARTER + 1):
        rows = q * MOE_QUARTER

        @pl.when(quarters == q)
        def _(rows=rows):
            if rows:
                _swiglu_tile(h_scr[0:rows, :], wg_ref, wu_ref, o_ref, rows)
            if rows < MOE_BLOCK:
                o_ref[rows:, :] = jnp.zeros((MOE_BLOCK - rows, o_ref.shape[1]), o_ref.dtype)


def _moe_up(xs, w_gate_up, li, blk_expert, n_live, blk_rows):
    d = xs.shape[1]
    n_blocks = blk_expert.shape[0]
    f = w_gate_up.shape[-1] // 2
    bm, bn = MOE_BLOCK, 512
    nt = f // bn

    def row(b, nl):
        return jnp.minimum(b, nl[0] - 1)

    def col(b, j, nl):
        return jnp.where(b < nl[0], j, nt - 1)

    grid_spec = pltpu.PrefetchScalarGridSpec(
        num_scalar_prefetch=3,
        grid=(n_blocks, nt),
        in_specs=[
            pl.BlockSpec((bm, d), lambda b, j, be, nl, br: (row(b, nl), 0)),
            pl.BlockSpec((None, None, d, bn),
                         lambda b, j, be, nl, br: (li, be[row(b, nl)], 0, col(b, j, nl))),
            pl.BlockSpec((None, None, d, bn),
                         lambda b, j, be, nl, br: (li, be[row(b, nl)], 0, nt + col(b, j, nl))),
        ],
        out_specs=pl.BlockSpec((bm, bn), lambda b, j, be, nl, br: (b, j)),
        scratch_shapes=[pltpu.VMEM((bm, d), BF16)],
    )
    return pl.pallas_call(
        _moe_up_kernel,
        out_shape=jax.ShapeDtypeStruct((n_blocks * bm, f), BF16),
        grid_spec=grid_spec,
        compiler_params=_params("arbitrary", "arbitrary"),
        name="moe_gate_up",
    )(blk_expert, n_live, blk_rows, xs, w_gate_up, w_gate_up)


def _moe_down_kernel(be_ref, nl_ref, rows_ref, a_ref, w_ref, o_ref):
    del be_ref, nl_ref
    quarters = _filled_quarters(rows_ref)

    for q in range(MOE_BLOCK // MOE_QUARTER + 1):
        rows = q * MOE_QUARTER

        @pl.when(quarters == q)
        def _(rows=rows):
            if rows:
                o_ref[0:rows, :] = jnp.dot(a_ref[0:rows, :], w_ref[...].astype(BF16),
                                           preferred_element_type=F32)
            if rows < MOE_BLOCK:
                o_ref[rows:, :] = jnp.zeros((MOE_BLOCK - rows, o_ref.shape[1]), o_ref.dtype)


def _moe_down(a, w_down, li, blk_expert, n_live, blk_rows):
    s, f = a.shape
    d = w_down.shape[-1]
    bm, bn = MOE_BLOCK, 256
    nt = d // bn

    def row(b, nl):
        return jnp.minimum(b, nl[0] - 1)

    def col(b, j, nl):
        return jnp.where(b < nl[0], j, nt - 1)

    grid_spec = pltpu.PrefetchScalarGridSpec(
        num_scalar_prefetch=3,
        grid=(s // bm, nt),
        in_specs=[
            pl.BlockSpec((bm, f), lambda b, j, be, nl, br: (row(b, nl), 0)),
            pl.BlockSpec((None, None, f, bn),
                         lambda b, j, be, nl, br: (li, be[row(b, nl)], 0, col(b, j, nl))),
        ],
        out_specs=pl.BlockSpec((bm, bn), lambda b, j, be, nl, br: (b, j)),
    )
    return pl.pallas_call(
        _moe_down_kernel,
        out_shape=jax.ShapeDtypeStruct((s, d), F32),
        grid_spec=grid_spec,
        compiler_params=_params("arbitrary", "arbitrary"),
        name="moe_down",
    )(blk_expert, n_live, blk_rows, a, w_down)


def _combine_kernel(s1_ref, s2_ref, x_ref, gate_ref, meta_ref, ys_ref, o_ref, g1, g2, sem):
    rows = x_ref.shape[0]
    chunk = 128

    def row_copy(i, slot_ref, dst, which):
        return pltpu.make_async_copy(ys_ref.at[pl.ds(slot_ref[0, i], 1)], dst.at[pl.ds(i, 1)],
                                     sem.at[which])

    def body(i, carry):
        row_copy(i, s1_ref, g1, 0).start(priority=0)
        row_copy(i, s2_ref, g2, 1).start(priority=1)
        return carry

    lax.fori_loop(0, rows, body, 0, unroll=8)
    pltpu.make_async_copy(ys_ref.at[pl.ds(0, rows)], g1, sem.at[0]).wait()
    pltpu.make_async_copy(ys_ref.at[pl.ds(0, rows)], g2, sem.at[1]).wait()

    gate = gate_ref[...]

    def mix(c, carry):
        r0 = pl.multiple_of(c * chunk, chunk)
        sl = pl.ds(r0, chunk)
        w1 = meta_ref[sl, 4:5]
        w2 = meta_ref[sl, 5:6]
        o_ref[sl, :] = x_ref[sl, :] + gate * (w1 * g1[sl, :] + w2 * g2[sl, :])
        return carry

    lax.fori_loop(0, rows // chunk, mix, 0)


def _combine(ys, slot1, slot2, meta, x2, gate, seq):
    t, d = x2.shape
    bm = ROUTE_BLOCK
    nblk = t // bm
    per_seq = seq // bm
    slots = [s.reshape(nblk, 1, bm) for s in (slot1, slot2)]
    smem_spec = pl.BlockSpec((None, 1, bm), lambda m: (m, 0, 0), memory_space=pltpu.SMEM)
    return pl.pallas_call(
        _combine_kernel,
        out_shape=jax.ShapeDtypeStruct((t, d), F32),
        grid=(nblk,),
        in_specs=[smem_spec, smem_spec,
                  pl.BlockSpec((bm, d), lambda m: (m, 0)),
                  pl.BlockSpec((None, 1, d), lambda m: (m // per_seq, 0, 0)),
                  pl.BlockSpec((bm, LANES), lambda m: (m, 0)),
                  pl.BlockSpec(memory_space=pl.ANY)],
        out_specs=pl.BlockSpec((bm, d), lambda m: (m, 0)),
        scratch_shapes=[pltpu.VMEM((bm, d), F32), pltpu.VMEM((bm, d), F32),
                        pltpu.SemaphoreType.DMA((2,))],
        compiler_params=_params("arbitrary"),
        name="moe_combine",
    )(slots[0], slots[1], x2, gate, meta, ys)


def _moe(x2, norm_gain, scale, shift, gate, w_router, w_gate_up, w_down, layer, li, seq):
    t, d = x2.shape
    e = w_router.shape[-1]
    blk = MOE_BLOCK
    h, meta, counts = _router(x2, norm_gain, scale, shift, w_router[li], layer, seq)

    cnt = counts[0, :e].astype(jnp.int32)
    nblk = (cnt + blk - 1) // blk
    blk_end = jnp.cumsum(nblk)
    blk_start = blk_end - nblk
    off = blk_start * blk
    n_blocks = pl.cdiv(TOP_K * t, blk) + e
    n_live = blk_end[-1:].astype(jnp.int32)
    block_ids = jnp.arange(n_blocks, dtype=jnp.int32)
    blk_expert = jnp.minimum(jnp.sum(block_ids[:, None] >= blk_end[None, :], axis=1), e - 1).astype(jnp.int32)
    blk_rows = jnp.clip(cnt[blk_expert] - (block_ids - blk_start[blk_expert]) * blk, 0, blk)
    blk_rows = jnp.where(block_ids < n_live[0], blk_rows, 0).astype(jnp.int32)
    ids = meta[:, 0:2].astype(jnp.int32)
    ranks = meta[:, 2:4].astype(jnp.int32)
    slots = off[ids] + ranks
    slot1, slot2 = slots[:, 0], slots[:, 1]

    n_slots = n_blocks * blk
    zero_end = jnp.concatenate([off[1:], jnp.full((1,), n_slots, jnp.int32)]).astype(jnp.int32)
    zero_chunks = ((zero_end - (off + cnt) + ZERO_ROWS - 1) // ZERO_ROWS).astype(jnp.int32)
    xs = _dispatch(h, slot1, slot2, zero_end, zero_chunks, n_slots)
    a = _moe_up(xs, w_gate_up, li, blk_expert, n_live, blk_rows)
    ys = _moe_down(a, w_down, li, blk_expert, n_live, blk_rows)
    return _combine(ys, slot1, slot2, meta, x2, gate, seq)


def kernel(x, c, ada_w, ada_b, norm_mix, w_in, q_norm, k_norm, attn_sinks, conv_w, w_attn_branch,
           w_conv_branch, w_out, norm_ffn, ffn_w_gate_up, ffn_w_down, moe_w_router, moe_w_gate_up,
           moe_w_down):
    batch, seq, d = x.shape
    depth = ada_w.shape[0]
    d_attn = w_attn_branch.shape[1]
    d_conv = w_conv_branch.shape[1]
    n_heads = d_attn // HEAD_DIM
    d_kv = (n_heads // GQA_GROUP) * HEAD_DIM
    col_b = d_attn + 2 * d_kv
    col_ga = col_b + 3 * d_conv
    col_gc = col_ga + d
    assert w_in.shape[-1] == col_gc + d and seq % ROW_BLOCK == 0
    assert 2 * HEAD_DIM == LANES and GQA_GROUP == 4 and (n_heads // GQA_GROUP) % 2 == 0

    n_mod = ada_w.shape[-1] // d
    mod = _adaln(c, ada_w, ada_b).reshape(depth, batch, n_mod, 1, d)
    bias = _alibi_bias(n_heads)
    x2 = x.reshape(batch * seq, d)

    for l in range(depth):
        shift_m, scale_m, gate_m, shift_f, scale_f, gate_f = (mod[l, :, i] for i in range(n_mod))

        reps_q, reps_k = d_attn // HEAD_DIM, d_kv // HEAD_DIM
        head_gain = jnp.concatenate([jnp.tile(q_norm[l] * HEAD_DIM ** -0.5, reps_q),
                                     jnp.tile(k_norm[l], reps_k), jnp.ones((d_kv,), F32)])[None]
        head_mask = jnp.concatenate([jnp.ones((d_attn + d_kv,), F32), jnp.zeros((d_kv,), F32)])[None]

        h_mix = _modnorm(x2, norm_mix.reshape(depth, 1, d), scale_m, shift_m, l, seq)
        y = _inproj(h_mix, w_in, l, head_gain, head_mask)
        attn = _attention(y, attn_sinks[l], bias, batch, seq, d_attn, d_kv)
        merged = _branch_merge(attn, y, conv_w, w_attn_branch, w_conv_branch, l, seq, col_b, col_ga, col_gc)
        x2 = _resid_proj(merged, w_out, l, x2, gate_m, seq, 2 * ROW_BLOCK, 512, "mixer_out_proj")

        norm_f = norm_ffn.reshape(depth, 1, d)
        if l % 2 == 0:
            a = _ffn_up(x2, norm_f, scale_f, shift_f, ffn_w_gate_up, l, l // 2, seq)
            x2 = _resid_proj(a, ffn_w_down, l // 2, x2, gate_f, seq, ROW_BLOCK, 256, "ffn_down")
        else:
            x2 = _moe(x2, norm_f, scale_f, shift_f, gate_f, moe_w_router, moe_w_gate_up, moe_w_down,
                      l, l // 2, seq)
    return x2.reshape(batch, seq, d)
```

```python
import functools

import jax
import jax.numpy as jnp
from jax import lax
from jax.experimental import pallas as pl
from jax.experimental.pallas import tpu as pltpu

F32 = jnp.float32
BF16 = jnp.bfloat16

HEAD_DIM = 64
GQA_GROUP = 4
ATTN_BLOCK = 128
TOP_K = 2
EPS = 1e-6

LANES = 128
BF16_ROWS = 16
V7X_VMEM_BYTES = 64 * 1024 * 1024
VMEM_LIMIT = V7X_VMEM_BYTES - 8 * 1024 * 1024

ROW_BLOCK = 1024
ROUTE_BLOCK = 512
MOE_BLOCK = ROW_BLOCK
MOE_QUARTER = MOE_BLOCK // 4
MOE_CHUNK = 128
ZERO_ROWS = 256


def _params(*sem):
    return pltpu.CompilerParams(dimension_semantics=sem, vmem_limit_bytes=VMEM_LIMIT)


def _sigmoid(x):
    return 0.5 + 0.5 * jnp.tanh(0.5 * x)


def _silu(x):
    half = 0.5 * x
    return half + half * jnp.tanh(half)


MXU_COLS = 256


def _swiglu_tile(h, wg_ref, wu_ref, o_ref, rows):
    for c0 in range(0, o_ref.shape[1], MXU_COLS):
        cols = slice(c0, c0 + MXU_COLS)
        g = jnp.dot(h, wg_ref[:, cols].astype(BF16), preferred_element_type=F32)
        u = jnp.dot(h, wu_ref[:, cols].astype(BF16), preferred_element_type=F32)
        o_ref[0:rows, cols] = (_silu(g) * u).astype(o_ref.dtype)


def _modulated_norm(x, gain_scale, shift):
    ms = jnp.mean(x * x, axis=-1, keepdims=True)
    return (x * lax.rsqrt(ms + EPS)) * gain_scale + shift


def _fill_modulated_norm(x_ref, gain_ref, scale_ref, shift_ref, h_ref, chunk=128):
    gain_scale = gain_ref[...] * (1.0 + scale_ref[...])
    shift = shift_ref[...]

    def body(c, carry):
        r0 = pl.multiple_of(c * chunk, chunk)
        h = _modulated_norm(x_ref[pl.ds(r0, chunk), :], gain_scale, shift)
        h_ref[pl.ds(r0, chunk), :] = h.astype(h_ref.dtype)
        return carry

    lax.fori_loop(0, x_ref.shape[0] // chunk, body, 0)


def _adaln_kernel(c_ref, w_ref, b_ref, o_ref):
    s = _silu(c_ref[...]).astype(BF16)
    o_ref[...] = jnp.dot(s, w_ref[...].astype(BF16), preferred_element_type=F32) + b_ref[...]


def _adaln(c, ada_w, ada_b):
    depth, d, n = ada_w.shape
    b = c.shape[0]
    rows = 8 * pl.cdiv(b, 8)
    c_pad = jnp.pad(c, ((0, rows - b), (0, 0)))
    bn = 1024
    out = pl.pallas_call(
        _adaln_kernel,
        out_shape=jax.ShapeDtypeStruct((depth, rows, n), F32),
        grid=(depth, n // bn),
        in_specs=[
            pl.BlockSpec((rows, d), lambda l, j: (0, 0)),
            pl.BlockSpec((None, d, bn), lambda l, j: (l, 0, j)),
            pl.BlockSpec((None, 1, bn), lambda l, j: (l, 0, j)),
        ],
        out_specs=pl.BlockSpec((None, rows, bn), lambda l, j: (l, 0, j)),
        compiler_params=_params("arbitrary", "arbitrary"),
        name="adaln_mod",
    )(c_pad, ada_w, ada_b.reshape(depth, 1, n))
    return out[:, :b]


def _modnorm_kernel(x_ref, gain_ref, scale_ref, shift_ref, o_ref):
    _fill_modulated_norm(x_ref, gain_ref, scale_ref, shift_ref, o_ref)


def _modnorm(x2, norm_gain, scale, shift, layer, seq):
    t, d = x2.shape
    bm = ROUTE_BLOCK
    per_seq = seq // bm
    return pl.pallas_call(
        _modnorm_kernel,
        out_shape=jax.ShapeDtypeStruct((t, d), BF16),
        grid=(t // bm,),
        in_specs=[
            pl.BlockSpec((bm, d), lambda m: (m, 0)),
            pl.BlockSpec((None, 1, d), lambda m: (layer, 0, 0)),
            pl.BlockSpec((None, 1, d), lambda m: (m // per_seq, 0, 0)),
            pl.BlockSpec((None, 1, d), lambda m: (m // per_seq, 0, 0)),
        ],
        out_specs=pl.BlockSpec((bm, d), lambda m: (m, 0)),
        compiler_params=_params("arbitrary"),
        name="mixer_modnorm",
    )(x2, norm_gain, scale, shift)


def _inproj_kernel(h_ref, w_ref, hg_ref, hm_ref, o_ref, *, norm_tiles):
    n = pl.program_id(1)
    acc = jnp.dot(h_ref[...], w_ref[...].astype(BF16), preferred_element_type=F32)

    @pl.when(n < norm_tiles)
    def _():
        bn = acc.shape[1]
        shift = HEAD_DIM.bit_length() - 1
        r = lax.shift_right_logical(lax.broadcasted_iota(jnp.int32, (bn, bn), 0), shift)
        c = lax.shift_right_logical(lax.broadcasted_iota(jnp.int32, (bn, bn), 1), shift)
        head_mean = jnp.where(r == c, 1.0 / HEAD_DIM, 0.0).astype(BF16)
        ms = jnp.dot((acc * acc).astype(BF16), head_mean, preferred_element_type=F32)
        normed = acc * lax.rsqrt(ms + EPS) * hg_ref[...]
        o_ref[...] = jnp.where(hm_ref[...] > 0.0, normed, acc).astype(o_ref.dtype)

    @pl.when(n >= norm_tiles)
    def _():
        o_ref[...] = acc.astype(o_ref.dtype)


def _inproj(h, w_in, layer, head_gain, head_mask):
    t, d = h.shape
    n = w_in.shape[-1]
    bm, bn = 2 * ROW_BLOCK, 512
    norm_cols = head_gain.shape[-1]
    kern = functools.partial(_inproj_kernel, norm_tiles=norm_cols // bn)
    return pl.pallas_call(
        kern,
        out_shape=jax.ShapeDtypeStruct((t, n), BF16),
        grid=(t // bm, n // bn),
        in_specs=[
            pl.BlockSpec((bm, d), lambda m, j: (m, 0)),
            pl.BlockSpec((None, d, bn), lambda m, j: (layer, 0, j)),
            pl.BlockSpec((1, bn), lambda m, j: (0, jnp.minimum(j, norm_cols // bn - 1))),
            pl.BlockSpec((1, bn), lambda m, j: (0, jnp.minimum(j, norm_cols // bn - 1))),
        ],
        out_specs=pl.BlockSpec((bm, bn), lambda m, j: (m, j)),
        compiler_params=_params("arbitrary", "arbitrary"),
        name="mixer_in_proj",
    )(h, w_in, head_gain, head_mask)


ATTN_STEP_BLOCKS = 2


def _attn_kernel(sinks_ref, q_ref, kvp_ref, kvc_ref, bias_ref, o_ref, *, n_kv):
    blk = ATTN_BLOCK
    first = pl.program_id(1) == 0
    dkv = n_kv * HEAD_DIM
    q = q_ref[...].astype(F32)
    kv = jnp.concatenate([kvp_ref[...], kvc_ref[...]], axis=0).astype(F32)
    k_all = kv[:, :dkv]
    v_all = kv[:, dkv:]
    lo = lax.broadcasted_iota(jnp.int32, (1, LANES), 1) < HEAD_DIM
    col = lax.broadcasted_iota(jnp.int32, (1, 2 * blk), 1)
    no_prev = jnp.where(jnp.logical_and(first, col < blk), -jnp.inf, 0.0).astype(F32)
    first_head = lax.broadcasted_iota(jnp.int32, (4 * blk, LANES), 0) < 2 * blk
    lo_full = lax.broadcasted_iota(jnp.int32, (4 * blk, LANES), 1) < HEAD_DIM
    sum_cols = jnp.where(first_head == lo_full, 1.0, 0.0).astype(BF16)

    for j in range(n_kv):
        t, half = divmod(j, 2)
        kp = k_all[:, LANES * t:LANES * (t + 1)]
        vp = v_all[:, LANES * t:LANES * (t + 1)]
        kr = pltpu.roll(kp, HEAD_DIM, 1)
        vr = pltpu.roll(vp, HEAD_DIM, 1)
        k_lo = jnp.where(lo, kr if half else kp, 0.0).astype(BF16)
        v_lo = jnp.where(lo, vr if half else vp, 0.0).astype(BF16)
        v_hi = jnp.where(lo, 0.0, vp if half else vr).astype(BF16)
        base = GQA_GROUP * HEAD_DIM * j
        for sb in range(ATTN_STEP_BLOCKS):
            rows = slice(blk * sb, blk * (sb + 1))
            keys = slice(blk * sb, blk * (sb + 2))
            v_stack = jnp.concatenate(
                [jnp.concatenate([v_lo[keys], v_hi[keys]], axis=0), sum_cols], axis=1)
            qa = q[rows, base:base + LANES]
            qb = q[rows, base + LANES:base + 2 * LANES]
            q_stack = jnp.concatenate(
                [qa, pltpu.roll(qa, HEAD_DIM, 1), qb, pltpu.roll(qb, HEAD_DIM, 1)], axis=0).astype(BF16)
            s = lax.dot_general(q_stack, k_lo[keys], (((1,), (1,)), ((), ())),
                                preferred_element_type=F32)
            probs, sink_w = [], []
            for g in range(GQA_GROUP):
                h = GQA_GROUP * j + g
                bias = bias_ref[h] + no_prev if sb == 0 else bias_ref[h]
                sg = s[blk * g:blk * (g + 1)] + bias
                sink = sinks_ref[h]
                m = jnp.maximum(jnp.max(sg, axis=-1, keepdims=True), sink)
                probs.append(jnp.exp(sg - m).astype(BF16))
                sink_w.append(jnp.exp(sink - m))
            for pair in range(GQA_GROUP // 2):
                pp = jnp.concatenate([probs[2 * pair], probs[2 * pair + 1]], axis=1)
                o = jnp.dot(pp, v_stack, preferred_element_type=F32)
                denom = o[:, LANES:] + jnp.where(lo, sink_w[2 * pair], sink_w[2 * pair + 1])
                o_ref[rows, base + LANES * pair:base + LANES * (pair + 1)] = (
                    o[:, :LANES] / denom).astype(o_ref.dtype)


def _alibi_bias(n_heads):
    blk = ATTN_BLOCK
    slopes = 2.0 ** (-8.0 * jnp.arange(1, n_heads + 1, dtype=F32) / n_heads)
    dist = (jnp.arange(blk) + blk)[:, None] - jnp.arange(2 * blk)[None, :]
    valid = (dist >= 0) & (dist < blk)
    bias = -(slopes[:, None, None] * dist.astype(F32)[None])
    return jnp.where(valid[None], bias, -jnp.inf)


def _attention(y, sinks, bias, batch, seq, d_attn, d_kv):
    t = y.shape[0]
    blk = ATTN_BLOCK
    step = ATTN_STEP_BLOCKS * blk
    ns = seq // step
    kv_w = 2 * d_kv
    kv_col = d_attn // kv_w
    kern = functools.partial(_attn_kernel, n_kv=d_kv // HEAD_DIM)
    grid_spec = pltpu.PrefetchScalarGridSpec(
        num_scalar_prefetch=1,
        grid=(batch, ns),
        in_specs=[
            pl.BlockSpec((step, d_attn), lambda b, i, s: (b * ns + i, 0)),
            pl.BlockSpec((blk, kv_w),
                         lambda b, i, s: ((b * ns + i) * ATTN_STEP_BLOCKS - jnp.minimum(i, 1), kv_col)),
            pl.BlockSpec((step, kv_w), lambda b, i, s: (b * ns + i, kv_col)),
            pl.BlockSpec(bias.shape, lambda b, i, s: (0, 0, 0)),
        ],
        out_specs=pl.BlockSpec((step, d_attn), lambda b, i, s: (b * ns + i, 0)),
    )
    return pl.pallas_call(
        kern,
        out_shape=jax.ShapeDtypeStruct((t, d_attn), BF16),
        grid_spec=grid_spec,
        compiler_params=_params("arbitrary", "arbitrary"),
        name="swa_attention",
    )(sinks, y, y, y, bias)


HALO = BF16_ROWS


def _branch_kernel(attn_ref, bc_ref, cx_ref, hbc_ref, hcx_ref, cw_ref, wa_ref, wc_ref, ga_ref, gc_ref,
                   o_ref, u_scr, conv_scr, *, blocks_per_seq, taps):
    m = pl.program_id(0)
    n = pl.program_id(1)
    dc = conv_scr.shape[1]
    half = dc // 2
    chunk = 64

    @pl.when(n == 0)
    def _():
        keep = jnp.where(m % blocks_per_seq == 0, 0.0, 1.0).astype(F32)
        hc = jnp.concatenate([hbc_ref[:, dc:], hcx_ref[:, :half]], axis=1).astype(F32)
        hx = hcx_ref[:, half:].astype(F32)
        u_scr[0:HALO, :] = hc * hx * keep

        def fill_u(c, carry):
            r0 = pl.multiple_of(c * chunk, chunk)
            cc = jnp.concatenate([bc_ref[pl.ds(r0, chunk), dc:], cx_ref[pl.ds(r0, chunk), :half]], axis=1)
            xx = cx_ref[pl.ds(r0, chunk), half:]
            u_scr[pl.ds(HALO + r0, chunk), :] = cc.astype(F32) * xx.astype(F32)
            return carry

        lax.fori_loop(0, conv_scr.shape[0] // chunk, fill_u, 0)

        def conv_rows(c, carry):
            r0 = pl.multiple_of(c * chunk, chunk)
            w0 = pl.multiple_of(r0 + HALO - 8, 8)
            win = u_scr[pl.ds(w0, chunk + 8), :]
            acc = cw_ref[taps - 1:taps, :] * win[8:]
            for back in range(1, taps):
                acc = acc + cw_ref[taps - 1 - back:taps - back, :] * pltpu.roll(win, back, 0)[8:]
            b = bc_ref[pl.ds(r0, chunk), :dc].astype(F32)
            conv_scr[pl.ds(r0, chunk), :] = (b * acc).astype(conv_scr.dtype)
            return carry

        lax.fori_loop(0, conv_scr.shape[0] // chunk, conv_rows, 0)

    a = jnp.dot(attn_ref[...], wa_ref[...].astype(BF16), preferred_element_type=F32)
    c = jnp.dot(conv_scr[...], wc_ref[...].astype(BF16), preferred_element_type=F32)
    merged = _sigmoid(ga_ref[...].astype(F32)) * a + _sigmoid(gc_ref[...].astype(F32)) * c
    o_ref[...] = merged.astype(o_ref.dtype)


def _branch_merge(attn, y, conv_w, w_attn, w_conv, layer, seq, col_b, col_ga, col_gc):
    t, d_attn = attn.shape
    dc = w_conv.shape[1]
    d = w_attn.shape[-1]
    taps = conv_w.shape[1]
    bm, bn = ROW_BLOCK, 512
    wide = dc + dc // 2
    assert col_b % wide == 0 and taps - 1 <= 8
    cb = col_b // wide
    halo_blocks = bm // HALO
    kern = functools.partial(_branch_kernel, blocks_per_seq=seq // bm, taps=taps)

    def halo_map(col):
        return lambda m, j: (jnp.maximum(m * halo_blocks - 1, 0), col)

    return pl.pallas_call(
        kern,
        out_shape=jax.ShapeDtypeStruct((t, d), BF16),
        grid=(t // bm, d // bn),
        in_specs=[
            pl.BlockSpec((bm, d_attn), lambda m, j: (m, 0)),
            pl.BlockSpec((bm, wide), lambda m, j: (m, cb)),
            pl.BlockSpec((bm, wide), lambda m, j: (m, cb + 1)),
            pl.BlockSpec((HALO, wide), halo_map(cb)),
            pl.BlockSpec((HALO, wide), halo_map(cb + 1)),
            pl.BlockSpec((None, taps, dc), lambda m, j: (layer, 0, 0)),
            pl.BlockSpec((None, d_attn, bn), lambda m, j: (layer, 0, j)),
            pl.BlockSpec((None, dc, bn), lambda m, j: (layer, 0, j)),
            pl.BlockSpec((bm, bn), lambda m, j: (m, col_ga // bn + j)),
            pl.BlockSpec((bm, bn), lambda m, j: (m, col_gc // bn + j)),
        ],
        out_specs=pl.BlockSpec((bm, bn), lambda m, j: (m, j)),
        scratch_shapes=[pltpu.VMEM((bm + HALO, dc), F32), pltpu.VMEM((bm, dc), BF16)],
        compiler_params=_params("arbitrary", "arbitrary"),
        name="branch_merge",
    )(attn, y, y, y, y, conv_w, w_attn, w_conv, y, y)


def _resid_proj_kernel(lhs_ref, w_ref, x_ref, gate_ref, o_ref):
    acc = jnp.dot(lhs_ref[...], w_ref[...].astype(BF16), preferred_element_type=F32)
    o_ref[...] = x_ref[...] + gate_ref[...] * acc


def _resid_proj(lhs, w, layer, x2, gate, seq, bm, bn, name):
    t, k = lhs.shape
    d = w.shape[-1]
    per_seq = seq // bm
    return pl.pallas_call(
        _resid_proj_kernel,
        out_shape=jax.ShapeDtypeStruct((t, d), F32),
        grid=(t // bm, d // bn),
        in_specs=[
            pl.BlockSpec((bm, k), lambda m, j: (m, 0)),
            pl.BlockSpec((None, k, bn), lambda m, j: (layer, 0, j)),
            pl.BlockSpec((bm, bn), lambda m, j: (m, j)),
            pl.BlockSpec((None, 1, bn), lambda m, j: (m // per_seq, 0, j)),
        ],
        out_specs=pl.BlockSpec((bm, bn), lambda m, j: (m, j)),
        compiler_params=_params("arbitrary", "arbitrary"),
        name=name,
    )(lhs, w, x2, gate)


def _ffn_up_kernel(x_ref, gain_ref, scale_ref, shift_ref, wg_ref, wu_ref, o_ref, h_scr):
    @pl.when(pl.program_id(1) == 0)
    def _():
        _fill_modulated_norm(x_ref, gain_ref, scale_ref, shift_ref, h_scr)

    _swiglu_tile(h_scr[...], wg_ref, wu_ref, o_ref, o_ref.shape[0])


def _ffn_up(x2, norm_gain, scale, shift, w_gate_up, layer, li, seq):
    t, d = x2.shape
    f = w_gate_up.shape[-1] // 2
    bm, bn = ROW_BLOCK, 512
    per_seq = seq // bm
    return pl.pallas_call(
        _ffn_up_kernel,
        out_shape=jax.ShapeDtypeStruct((t, f), BF16),
        grid=(t // bm, f // bn),
        in_specs=[
            pl.BlockSpec((bm, d), lambda m, j: (m, 0)),
            pl.BlockSpec((None, 1, d), lambda m, j: (layer, 0, 0)),
            pl.BlockSpec((None, 1, d), lambda m, j: (m // per_seq, 0, 0)),
            pl.BlockSpec((None, 1, d), lambda m, j: (m // per_seq, 0, 0)),
            pl.BlockSpec((None, d, bn), lambda m, j: (li, 0, j)),
            pl.BlockSpec((None, d, bn), lambda m, j: (li, 0, f // bn + j)),
        ],
        out_specs=pl.BlockSpec((bm, bn), lambda m, j: (m, j)),
        scratch_shapes=[pltpu.VMEM((bm, d), BF16)],
        compiler_params=_params("arbitrary", "arbitrary"),
        name="ffn_gate_up",
    )(x2, norm_gain, scale, shift, w_gate_up, w_gate_up)


def _router_kernel(x_ref, gain_ref, scale_ref, shift_ref, rw_ref, h_ref, meta_ref, cnt_ref,
                   tri_scr, base_scr, hl_scr, *, n_experts):
    i = pl.program_id(0)
    rows = x_ref.shape[0]
    chunk = 64

    @pl.when(i == 0)
    def _():
        r = lax.broadcasted_iota(jnp.int32, (rows, rows), 0)
        c = lax.broadcasted_iota(jnp.int32, (rows, rows), 1)
        tri_scr[...] = jnp.where(c < r, 1.0, 0.0).astype(BF16)
        base_scr[...] = jnp.zeros(base_scr.shape, F32)

    gain_scale = gain_ref[...] * (1.0 + scale_ref[...])
    shift = shift_ref[...]

    def body(c, carry):
        r0 = pl.multiple_of(c * chunk, chunk)
        h = _modulated_norm(x_ref[pl.ds(r0, chunk), :], gain_scale, shift)
        h_ref[pl.ds(r0, chunk), :] = h
        hi = h.astype(BF16)
        hl_scr[pl.ds(r0, chunk), :] = hi
        hl_scr[pl.ds(pl.multiple_of(rows + r0, chunk), chunk), :] = (h - hi.astype(F32)).astype(BF16)
        return carry

    lax.fori_loop(0, rows // chunk, body, 0)

    prod = jnp.dot(hl_scr[...], rw_ref[...], preferred_element_type=F32)
    lg = (prod[:rows, :LANES] + prod[:rows, LANES:]) + (prod[rows:, :LANES] + prod[rows:, LANES:])
    lane_i = lax.broadcasted_iota(jnp.int32, lg.shape, 1)
    lg = jnp.where(lane_i < n_experts, lg, -jnp.inf)
    lane = lane_i.astype(F32)
    m1 = jnp.max(lg, axis=-1, keepdims=True)
    i1 = jnp.min(jnp.where(lg == m1, lane, float(LANES)), axis=-1, keepdims=True)
    lg2 = jnp.where(lane == i1, -jnp.inf, lg)
    m2 = jnp.max(lg2, axis=-1, keepdims=True)
    i2 = jnp.min(jnp.where(lg2 == m2, lane, float(LANES)), axis=-1, keepdims=True)
    ex = jnp.exp(m2 - m1)
    w1 = 1.0 / (1.0 + ex)
    w2 = ex / (1.0 + ex)
    onehot = jnp.where(jnp.logical_or(lane == i1, lane == i2), 1.0, 0.0)
    ranks = jnp.dot(tri_scr[...], onehot.astype(BF16), preferred_element_type=F32) + base_scr[...]
    r1 = jnp.sum(jnp.where(lane == i1, ranks, 0.0), axis=-1, keepdims=True)
    r2 = jnp.sum(jnp.where(lane == i2, ranks, 0.0), axis=-1, keepdims=True)
    meta = jnp.zeros(lg.shape, F32)
    for pos, val in enumerate((i1, i2, r1, r2, w1, w2)):
        meta = jnp.where(lane == float(pos), val, meta)
    meta_ref[...] = meta
    new_base = base_scr[...] + jnp.sum(onehot, axis=0, keepdims=True)
    base_scr[...] = new_base
    cnt_ref[...] = new_base


def _router(x2, norm_gain, scale, shift, w_router, layer, seq):
    t, d = x2.shape
    e = w_router.shape[-1]
    bm = ROUTE_BLOCK
    per_seq = seq // bm
    kern = functools.partial(_router_kernel, n_experts=e)
    w_pad = jnp.pad(w_router, ((0, 0), (0, LANES - e)))
    w_hi = w_pad.astype(BF16)
    w_lo = (w_pad - w_hi.astype(F32)).astype(BF16)
    w_hl = jnp.concatenate([w_hi, w_lo], axis=1)
    return pl.pallas_call(
        kern,
        out_shape=(jax.ShapeDtypeStruct((t, d), F32),
                   jax.ShapeDtypeStruct((t, LANES), F32),
                   jax.ShapeDtypeStruct((1, LANES), F32)),
        grid=(t // bm,),
        in_specs=[
            pl.BlockSpec((bm, d), lambda m: (m, 0)),
            pl.BlockSpec((None, 1, d), lambda m: (layer, 0, 0)),
            pl.BlockSpec((None, 1, d), lambda m: (m // per_seq, 0, 0)),
            pl.BlockSpec((None, 1, d), lambda m: (m // per_seq, 0, 0)),
            pl.BlockSpec((d, 2 * LANES), lambda m: (0, 0)),
        ],
        out_specs=(pl.BlockSpec((bm, d), lambda m: (m, 0)),
                   pl.BlockSpec((bm, LANES), lambda m: (m, 0)),
                   pl.BlockSpec((1, LANES), lambda m: (0, 0))),
        scratch_shapes=[pltpu.VMEM((bm, bm), BF16), pltpu.VMEM((1, LANES), F32),
                        pltpu.VMEM((2 * bm, d), BF16)],
        compiler_params=_params("arbitrary"),
        name="moe_router",
    )(x2, norm_gain, scale, shift, w_hl)


def _dispatch_kernel(zend_ref, znum_ref, s1_ref, s2_ref, h_ref, xs_ref, zero_scr, sem, zsem):
    rows = h_ref.shape[0]

    @pl.when(pl.program_id(0) == 0)
    def _():
        zero_scr[...] = jnp.zeros(zero_scr.shape, zero_scr.dtype)

        for e in range(zend_ref.shape[0]):
            def fill(k, e=e):
                start = pl.multiple_of(zend_ref[e] - (k + 1) * ZERO_ROWS, ZERO_ROWS)
                return pltpu.make_async_copy(zero_scr, xs_ref.at[pl.ds(start, ZERO_ROWS)], zsem)

            def start_fill(k, carry, fill=fill):
                fill(k).start()
                return carry

            def wait_fill(k, carry, fill=fill):
                fill(k).wait()
                return carry

            lax.fori_loop(0, znum_ref[e], start_fill, 0)
            lax.fori_loop(0, znum_ref[e], wait_fill, 0)

    def row_copy(i, slot_ref):
        return pltpu.make_async_copy(h_ref.at[pl.ds(i, 1)], xs_ref.at[pl.ds(slot_ref[0, i], 1)], sem)

    def body(i, carry):
        row_copy(i, s1_ref).start(priority=0)
        row_copy(i, s2_ref).start(priority=1)
        return carry

    lax.fori_loop(0, rows, body, 0, unroll=8)
    for _ in range(TOP_K):
        pltpu.make_async_copy(h_ref, xs_ref.at[pl.ds(0, rows)], sem).wait()


def _dispatch(h, slot1, slot2, zero_end, zero_chunks, n_slots):
    t, d = h.shape
    bm = ROW_BLOCK
    nblk = t // bm
    slots = [s.reshape(nblk, 1, bm) for s in (slot1, slot2)]
    smem_spec = pl.BlockSpec((None, 1, bm), lambda m, zs, zn: (m, 0, 0), memory_space=pltpu.SMEM)
    grid_spec = pltpu.PrefetchScalarGridSpec(
        num_scalar_prefetch=2,
        grid=(nblk,),
        in_specs=[smem_spec, smem_spec, pl.BlockSpec((bm, d), lambda m, zs, zn: (m, 0))],
        out_specs=pl.BlockSpec(memory_space=pl.ANY),
        scratch_shapes=[pltpu.VMEM((ZERO_ROWS, d), F32), pltpu.SemaphoreType.DMA(()),
                        pltpu.SemaphoreType.DMA(())],
    )
    return pl.pallas_call(
        _dispatch_kernel,
        out_shape=jax.ShapeDtypeStruct((n_slots, d), F32),
        grid_spec=grid_spec,
        compiler_params=_params("arbitrary"),
        name="moe_dispatch",
    )(zero_end, zero_chunks, slots[0], slots[1], h)


def _filled_quarters(rows_ref):
    shift = MOE_QUARTER.bit_length() - 1
    return lax.shift_right_logical(rows_ref[pl.program_id(0)] + (MOE_QUARTER - 1), shift)


def _moe_up_kernel(be_ref, nl_ref, rows_ref, xs_ref, wg_ref, wu_ref, o_ref, h_scr):
    del be_ref, nl_ref
    quarters = _filled_quarters(rows_ref)
    chunk = MOE_CHUNK

    @pl.when(jnp.logical_and(quarters > 0, pl.program_id(1) == 0))
    def _():
        def body(c, carry):
            r0 = pl.multiple_of(c * chunk, chunk)
            h_scr[pl.ds(r0, chunk), :] = xs_ref[pl.ds(r0, chunk), :].astype(h_scr.dtype)
            return carry

        lax.fori_loop(0, quarters * (MOE_QUARTER // chunk), body, 0)

    for q in range(MOE_BLOCK // MOE_QUARTER + 1):
        rows = q * MOE_QUARTER

        @pl.when(quarters == q)
        def _(rows=rows):
            if rows:
                _swiglu_tile(h_scr[0:rows, :], wg_ref, wu_ref, o_ref, rows)
            if rows < MOE_BLOCK:
                o_ref[rows:, :] = jnp.zeros((MOE_BLOCK - rows, o_ref.shape[1]), o_ref.dtype)


def _moe_up(xs, w_gate_up, li, blk_expert, n_live, blk_rows):
    d = xs.shape[1]
    n_blocks = blk_expert.shape[0]
    f = w_gate_up.shape[-1] // 2
    bm, bn = MOE_BLOCK, 512
    nt = f // bn

    def row(b, nl):
        return jnp.minimum(b, nl[0] - 1)

    def col(b, j, nl):
        return jnp.where(b < nl[0], j, nt - 1)

    grid_spec = pltpu.PrefetchScalarGridSpec(
        num_scalar_prefetch=3,
        grid=(n_blocks, nt),
        in_specs=[
            pl.BlockSpec((bm, d), lambda b, j, be, nl, br: (row(b, nl), 0)),
            pl.BlockSpec((None, None, d, bn),
                         lambda b, j, be, nl, br: (li, be[row(b, nl)], 0, col(b, j, nl))),
            pl.BlockSpec((None, None, d, bn),
                         lambda b, j, be, nl, br: (li, be[row(b, nl)], 0, nt + col(b, j, nl))),
        ],
        out_specs=pl.BlockSpec((bm, bn), lambda b, j, be, nl, br: (b, j)),
        scratch_shapes=[pltpu.VMEM((bm, d), BF16)],
    )
    return pl.pallas_call(
        _moe_up_kernel,
        out_shape=jax.ShapeDtypeStruct((n_blocks * bm, f), BF16),
        grid_spec=grid_spec,
        compiler_params=_params("arbitrary", "arbitrary"),
        name="moe_gate_up",
    )(blk_expert, n_live, blk_rows, xs, w_gate_up, w_gate_up)


def _moe_down_kernel(be_ref, nl_ref, rows_ref, a_ref, w_ref, o_ref):
    del be_ref, nl_ref
    quarters = _filled_quarters(rows_ref)

    for q in range(MOE_BLOCK // MOE_QUARTER + 1):
        rows = q * MOE_QUARTER

        @pl.when(quarters == q)
        def _(rows=rows):
            if rows:
                o_ref[0:rows, :] = jnp.dot(a_ref[0:rows, :], w_ref[...].astype(BF16),
                                           preferred_element_type=F32)
            if rows < MOE_BLOCK:
                o_ref[rows:, :] = jnp.zeros((MOE_BLOCK - rows, o_ref.shape[1]), o_ref.dtype)


def _moe_down(a, w_down, li, blk_expert, n_live, blk_rows):
    s, f = a.shape
    d = w_down.shape[-1]
    bm, bn = MOE_BLOCK, 256
    nt = d // bn

    def row(b, nl):
        return jnp.minimum(b, nl[0] - 1)

    def col(b, j, nl):
        return jnp.where(b < nl[0], j, nt - 1)

    grid_spec = pltpu.PrefetchScalarGridSpec(
        num_scalar_prefetch=3,
        grid=(s // bm, nt),
        in_specs=[
            pl.BlockSpec((bm, f), lambda b, j, be, nl, br: (row(b, nl), 0)),
            pl.BlockSpec((None, None, f, bn),
                         lambda b, j, be, nl, br: (li, be[row(b, nl)], 0, col(b, j, nl))),
        ],
        out_specs=pl.BlockSpec((bm, bn), lambda b, j, be, nl, br: (b, j)),
    )
    return pl.pallas_call(
        _moe_down_kernel,
        out_shape=jax.ShapeDtypeStruct((s, d), F32),
        grid_spec=grid_spec,
        compiler_params=_params("arbitrary", "arbitrary"),
        name="moe_down",
    )(blk_expert, n_live, blk_rows, a, w_down)


def _combine_kernel(s1_ref, s2_ref, n1_ref, n2_ref, x_ref, gate_ref, meta_ref, ys_ref, o_ref, gbuf, sem):
    m = pl.program_id(0)
    rows = x_ref.shape[0]
    chunk = 128

    def issue(par, first_ref, second_ref):
        def row_copy(i, slot_ref, k):
            return pltpu.make_async_copy(ys_ref.at[pl.ds(slot_ref[0, i], 1)],
                                         gbuf.at[par, k, pl.ds(i, 1)], sem.at[par, k])

        def body(g, carry):
            r0 = pl.multiple_of(g * 8, 8)
            for k in range(8):
                row_copy(r0 + k, first_ref, 0).start(priority=0)
                row_copy(r0 + k, second_ref, 1).start(priority=1)
            return carry

        lax.fori_loop(0, rows // 8, body, 0)

    def step(par):
        @pl.when(m == 0)
        def _():
            issue(par, s1_ref, s2_ref)

        @pl.when(m + 1 < pl.num_programs(0))
        def _():
            issue(1 - par, n1_ref, n2_ref)

        for k in range(TOP_K):
            pltpu.make_async_copy(ys_ref.at[pl.ds(0, rows)], gbuf.at[par, k], sem.at[par, k]).wait()

        gate = gate_ref[...]

        def mix(c, carry):
            sl = pl.ds(pl.multiple_of(c * chunk, chunk), chunk)
            w1 = meta_ref[sl, 4:5]
            w2 = meta_ref[sl, 5:6]
            o_ref[sl, :] = x_ref[sl, :] + gate * (w1 * gbuf[par, 0, sl, :] + w2 * gbuf[par, 1, sl, :])
            return carry

        lax.fori_loop(0, rows // chunk, mix, 0)

    for par in range(2):
        pl.when(m % 2 == par)(functools.partial(step, par))


def _combine(ys, slot1, slot2, meta, x2, gate, seq):
    t, d = x2.shape
    bm = ROUTE_BLOCK
    nblk = t // bm
    per_seq = seq // bm
    slots = [s.reshape(nblk, 1, bm) for s in (slot1, slot2)]
    smem_spec = pl.BlockSpec((None, 1, bm), lambda m: (m, 0, 0), memory_space=pltpu.SMEM)
    next_spec = pl.BlockSpec((None, 1, bm), lambda m: (jnp.minimum(m + 1, nblk - 1), 0, 0),
                             memory_space=pltpu.SMEM)
    return pl.pallas_call(
        _combine_kernel,
        out_shape=jax.ShapeDtypeStruct((t, d), F32),
        grid=(nblk,),
        in_specs=[smem_spec, smem_spec, next_spec, next_spec,
                  pl.BlockSpec((bm, d), lambda m: (m, 0)),
                  pl.BlockSpec((None, 1, d), lambda m: (m // per_seq, 0, 0)),
                  pl.BlockSpec((bm, LANES), lambda m: (m, 0)),
                  pl.BlockSpec(memory_space=pl.ANY)],
        out_specs=pl.BlockSpec((bm, d), lambda m: (m, 0)),
        scratch_shapes=[pltpu.VMEM((2, TOP_K, bm, d), F32), pltpu.SemaphoreType.DMA((2, TOP_K))],
        compiler_params=_params("arbitrary"),
        name="moe_combine",
    )(slots[0], slots[1], slots[0], slots[1], x2, gate, meta, ys)


def _moe(x2, norm_gain, scale, shift, gate, w_router, w_gate_up, w_down, layer, li, seq):
    t, d = x2.shape
    e = w_router.shape[-1]
    blk = MOE_BLOCK
    h, meta, counts = _router(x2, norm_gain, scale, shift, w_router[li], layer, seq)

    cnt = counts[0, :e].astype(jnp.int32)
    nblk = (cnt + blk - 1) // blk
    blk_end = jnp.cumsum(nblk)
    blk_start = blk_end - nblk
    off = blk_start * blk
    n_blocks = pl.cdiv(TOP_K * t, blk) + e
    n_live = blk_end[-1:].astype(jnp.int32)
    block_ids = jnp.arange(n_blocks, dtype=jnp.int32)
    blk_expert = jnp.minimum(jnp.sum(block_ids[:, None] >= blk_end[None, :], axis=1), e - 1).astype(jnp.int32)
    blk_rows = jnp.clip(cnt[blk_expert] - (block_ids - blk_start[blk_expert]) * blk, 0, blk)
    blk_rows = jnp.where(block_ids < n_live[0], blk_rows, 0).astype(jnp.int32)
    ids = meta[:, 0:2].astype(jnp.int32)
    ranks = meta[:, 2:4].astype(jnp.int32)
    slots = off[ids] + ranks
    slot1, slot2 = slots[:, 0], slots[:, 1]

    n_slots = n_blocks * blk
    zero_end = jnp.concatenate([off[1:], jnp.full((1,), n_slots, jnp.int32)]).astype(jnp.int32)
    zero_chunks = ((zero_end - (off + cnt) + ZERO_ROWS - 1) // ZERO_ROWS).astype(jnp.int32)
    xs = _dispatch(h, slot1, slot2, zero_end, zero_chunks, n_slots)
    a = _moe_up(xs, w_gate_up, li, blk_expert, n_live, blk_rows)
    ys = _moe_down(a, w_down, li, blk_expert, n_live, blk_rows)
    return _combine(ys, slot1, slot2, meta, x2, gate, seq)


def kernel(x, c, ada_w, ada_b, norm_mix, w_in, q_norm, k_norm, attn_sinks, conv_w, w_attn_branch,
           w_conv_branch, w_out, norm_ffn, ffn_w_gate_up, ffn_w_down, moe_w_router, moe_w_gate_up,
           moe_w_down):
    batch, seq, d = x.shape
    depth = ada_w.shape[0]
    d_attn = w_attn_branch.shape[1]
    d_conv = w_conv_branch.shape[1]
    n_heads = d_attn // HEAD_DIM
    d_kv = (n_heads // GQA_GROUP) * HEAD_DIM
    col_b = d_attn + 2 * d_kv
    col_ga = col_b + 3 * d_conv
    col_gc = col_ga + d
    assert w_in.shape[-1] == col_gc + d and seq % ROW_BLOCK == 0
    assert 2 * HEAD_DIM == LANES and GQA_GROUP == 4 and (n_heads // GQA_GROUP) % 2 == 0

    n_mod = ada_w.shape[-1] // d
    mod = _adaln(c, ada_w, ada_b).reshape(depth, batch, n_mod, 1, d)
    bias = _alibi_bias(n_heads)
    x2 = x.reshape(batch * seq, d)

    for l in range(depth):
        shift_m, scale_m, gate_m, shift_f, scale_f, gate_f = (mod[l, :, i] for i in range(n_mod))

        reps_q, reps_k = d_attn // HEAD_DIM, d_kv // HEAD_DIM
        head_gain = jnp.concatenate([jnp.tile(q_norm[l] * HEAD_DIM ** -0.5, reps_q),
                                     jnp.tile(k_norm[l], reps_k), jnp.ones((d_kv,), F32)])[None]
        head_mask = jnp.concatenate([jnp.ones((d_attn + d_kv,), F32), jnp.zeros((d_kv,), F32)])[None]

        h_mix = _modnorm(x2, norm_mix.reshape(depth, 1, d), scale_m, shift_m, l, seq)
        y = _inproj(h_mix, w_in, l, head_gain, head_mask)
        attn = _attention(y, attn_sinks[l], bias, batch, seq, d_attn, d_kv)
        merged = _branch_merge(attn, y, conv_w, w_attn_branch, w_conv_branch, l, seq, col_b, col_ga, col_gc)
        x2 = _resid_proj(merged, w_out, l, x2, gate_m, seq, 2 * ROW_BLOCK, 512, "mixer_out_proj")

        norm_f = norm_ffn.reshape(depth, 1, d)
        if l % 2 == 0:
            a = _ffn_up(x2, norm_f, scale_f, shift_f, ffn_w_gate_up, l, l // 2, seq)
            x2 = _resid_proj(a, ffn_w_down, l // 2, x2, gate_f, seq, ROW_BLOCK, 256, "ffn_down")
        else:
            x2 = _moe(x2, norm_f, scale_f, shift_f, gate_f, moe_w_router, moe_w_gate_up, moe_w_down,
                      l, l // 2, seq)
    return x2.reshape(batch, seq, d)
```

```python
import functools

import jax
import jax.numpy as jnp
from jax import lax
from jax.experimental import pallas as pl
from jax.experimental.pallas import tpu as pltpu

F32 = jnp.float32
BF16 = jnp.bfloat16

HEAD_DIM = 64
GQA_GROUP = 4
ATTN_BLOCK = 128
TOP_K = 2
EPS = 1e-6

LANES = 128
BF16_ROWS = 16
V7X_VMEM_BYTES = 64 * 1024 * 1024
VMEM_LIMIT = V7X_VMEM_BYTES - 8 * 1024 * 1024

ROW_BLOCK = 1024
ROUTE_BLOCK = 512
MOE_BLOCK = ROW_BLOCK
MOE_QUARTER = MOE_BLOCK // 4
MOE_CHUNK = 128
ZERO_ROWS = 256


def _params(*sem):
    return pltpu.CompilerParams(dimension_semantics=sem, vmem_limit_bytes=VMEM_LIMIT)


def _sigmoid(x):
    return 0.5 + 0.5 * jnp.tanh(0.5 * x)


def _silu(x):
    half = 0.5 * x
    return half + half * jnp.tanh(half)


MXU_COLS = 256


def _swiglu_tile(h, wg_ref, wu_ref, o_ref, rows):
    for c0 in range(0, o_ref.shape[1], MXU_COLS):
        cols = slice(c0, c0 + MXU_COLS)
        g = jnp.dot(h, wg_ref[:, cols].astype(BF16), preferred_element_type=F32)
        u = jnp.dot(h, wu_ref[:, cols].astype(BF16), preferred_element_type=F32)
        o_ref[0:rows, cols] = (_silu(g) * u).astype(o_ref.dtype)


def _modulated_norm(x, gain_scale, shift):
    ms = jnp.mean(x * x, axis=-1, keepdims=True)
    return (x * lax.rsqrt(ms + EPS)) * gain_scale + shift


def _fill_modulated_norm(x_ref, gain_ref, scale_ref, shift_ref, h_ref, chunk=128):
    gain_scale = gain_ref[...] * (1.0 + scale_ref[...])
    shift = shift_ref[...]

    def body(c, carry):
        r0 = pl.multiple_of(c * chunk, chunk)
        h = _modulated_norm(x_ref[pl.ds(r0, chunk), :], gain_scale, shift)
        h_ref[pl.ds(r0, chunk), :] = h.astype(h_ref.dtype)
        return carry

    lax.fori_loop(0, x_ref.shape[0] // chunk, body, 0)


def _cast_kernel(x_ref, o_ref, *, chunk):
    def body(c, carry):
        sl = pl.ds(pl.multiple_of(c * chunk, chunk), chunk)
        o_ref[sl, :] = x_ref[sl, :].astype(o_ref.dtype)
        return carry

    lax.fori_loop(0, x_ref.shape[0] // chunk, body, 0)


def _to_bf16(w, row_blocks, chunks):
    n, r, c = w.shape
    br = r // row_blocks
    assert br * row_blocks == r and br % (chunks * BF16_ROWS) == 0
    spec = pl.BlockSpec((None, br, c), lambda i, a: (i, a, 0))
    return pl.pallas_call(
        functools.partial(_cast_kernel, chunk=br // chunks),
        out_shape=jax.ShapeDtypeStruct(w.shape, BF16),
        grid=(n, row_blocks),
        in_specs=[spec],
        out_specs=spec,
        compiler_params=_params("arbitrary", "arbitrary"),
        name="weight_to_bf16",
    )(w)


def _adaln_kernel(c_ref, w_ref, b_ref, o_ref):
    s = _silu(c_ref[...]).astype(BF16)
    o_ref[...] = jnp.dot(s, w_ref[...].astype(BF16), preferred_element_type=F32) + b_ref[...]


def _adaln(c, ada_w, ada_b):
    depth, d, n = ada_w.shape
    b = c.shape[0]
    rows = 8 * pl.cdiv(b, 8)
    c_pad = jnp.pad(c, ((0, rows - b), (0, 0)))
    bn = 1024
    out = pl.pallas_call(
        _adaln_kernel,
        out_shape=jax.ShapeDtypeStruct((depth, rows, n), F32),
        grid=(depth, n // bn),
        in_specs=[
            pl.BlockSpec((rows, d), lambda l, j: (0, 0)),
            pl.BlockSpec((None, d, bn), lambda l, j: (l, 0, j)),
            pl.BlockSpec((None, 1, bn), lambda l, j: (l, 0, j)),
        ],
        out_specs=pl.BlockSpec((None, rows, bn), lambda l, j: (l, 0, j)),
        compiler_params=_params("arbitrary", "arbitrary"),
        name="adaln_mod",
    )(c_pad, ada_w, ada_b.reshape(depth, 1, n))
    return out[:, :b]


def _modnorm_kernel(x_ref, gain_ref, scale_ref, shift_ref, o_ref):
    _fill_modulated_norm(x_ref, gain_ref, scale_ref, shift_ref, o_ref)


def _modnorm(x2, norm_gain, scale, shift, layer, seq):
    t, d = x2.shape
    bm = ROUTE_BLOCK
    per_seq = seq // bm
    return pl.pallas_call(
        _modnorm_kernel,
        out_shape=jax.ShapeDtypeStruct((t, d), BF16),
        grid=(t // bm,),
        in_specs=[
            pl.BlockSpec((bm, d), lambda m: (m, 0)),
            pl.BlockSpec((None, 1, d), lambda m: (layer, 0, 0)),
            pl.BlockSpec((None, 1, d), lambda m: (m // per_seq, 0, 0)),
            pl.BlockSpec((None, 1, d), lambda m: (m // per_seq, 0, 0)),
        ],
        out_specs=pl.BlockSpec((bm, d), lambda m: (m, 0)),
        compiler_params=_params("arbitrary"),
        name="mixer_modnorm",
    )(x2, norm_gain, scale, shift)


def _inproj_kernel(h_ref, w_ref, hg_ref, hm_ref, o_ref, *, norm_tiles):
    n = pl.program_id(1)
    acc = jnp.dot(h_ref[...], w_ref[...].astype(BF16), preferred_element_type=F32)

    @pl.when(n < norm_tiles)
    def _():
        bn = acc.shape[1]
        shift = HEAD_DIM.bit_length() - 1
        r = lax.shift_right_logical(lax.broadcasted_iota(jnp.int32, (bn, bn), 0), shift)
        c = lax.shift_right_logical(lax.broadcasted_iota(jnp.int32, (bn, bn), 1), shift)
        head_mean = jnp.where(r == c, 1.0 / HEAD_DIM, 0.0).astype(BF16)
        ms = jnp.dot((acc * acc).astype(BF16), head_mean, preferred_element_type=F32)
        normed = acc * lax.rsqrt(ms + EPS) * hg_ref[...]
        o_ref[...] = jnp.where(hm_ref[...] > 0.0, normed, acc).astype(o_ref.dtype)

    @pl.when(n >= norm_tiles)
    def _():
        o_ref[...] = acc.astype(o_ref.dtype)


def _inproj(h, w_in, layer, head_gain, head_mask):
    t, d = h.shape
    n = w_in.shape[-1]
    bm, bn = 2 * ROW_BLOCK, 512
    norm_cols = head_gain.shape[-1]
    kern = functools.partial(_inproj_kernel, norm_tiles=norm_cols // bn)
    return pl.pallas_call(
        kern,
        out_shape=jax.ShapeDtypeStruct((t, n), BF16),
        grid=(t // bm, n // bn),
        in_specs=[
            pl.BlockSpec((bm, d), lambda m, j: (m, 0)),
            pl.BlockSpec((None, d, bn), lambda m, j: (layer, 0, j)),
            pl.BlockSpec((1, bn), lambda m, j: (0, jnp.minimum(j, norm_cols // bn - 1))),
            pl.BlockSpec((1, bn), lambda m, j: (0, jnp.minimum(j, norm_cols // bn - 1))),
        ],
        out_specs=pl.BlockSpec((bm, bn), lambda m, j: (m, j)),
        compiler_params=_params("arbitrary", "arbitrary"),
        name="mixer_in_proj",
    )(h, w_in, head_gain, head_mask)


ATTN_STEP_BLOCKS = 2


def _attn_kernel(sinks_ref, q_ref, kvp_ref, kvc_ref, bias_ref, o_ref, *, n_kv):
    blk = ATTN_BLOCK
    first = pl.program_id(1) == 0
    dkv = n_kv * HEAD_DIM
    q = q_ref[...].astype(F32)
    kv = jnp.concatenate([kvp_ref[...], kvc_ref[...]], axis=0).astype(F32)
    k_all = kv[:, :dkv]
    v_all = kv[:, dkv:]
    lo = lax.broadcasted_iota(jnp.int32, (1, LANES), 1) < HEAD_DIM
    col = lax.broadcasted_iota(jnp.int32, (1, 2 * blk), 1)
    no_prev = jnp.where(jnp.logical_and(first, col < blk), -jnp.inf, 0.0).astype(F32)
    first_head = lax.broadcasted_iota(jnp.int32, (4 * blk, LANES), 0) < 2 * blk
    lo_full = lax.broadcasted_iota(jnp.int32, (4 * blk, LANES), 1) < HEAD_DIM
    sum_cols = jnp.where(first_head == lo_full, 1.0, 0.0).astype(BF16)

    for j in range(n_kv):
        t, half = divmod(j, 2)
        kp = k_all[:, LANES * t:LANES * (t + 1)]
        vp = v_all[:, LANES * t:LANES * (t + 1)]
        kr = pltpu.roll(kp, HEAD_DIM, 1)
        vr = pltpu.roll(vp, HEAD_DIM, 1)
        k_lo = jnp.where(lo, kr if half else kp, 0.0).astype(BF16)
        v_lo = jnp.where(lo, vr if half else vp, 0.0).astype(BF16)
        v_hi = jnp.where(lo, 0.0, vp if half else vr).astype(BF16)
        base = GQA_GROUP * HEAD_DIM * j
        for sb in range(ATTN_STEP_BLOCKS):
            rows = slice(blk * sb, blk * (sb + 1))
            keys = slice(blk * sb, blk * (sb + 2))
            v_stack = jnp.concatenate(
                [jnp.concatenate([v_lo[keys], v_hi[keys]], axis=0), sum_cols], axis=1)
            qa = q[rows, base:base + LANES]
            qb = q[rows, base + LANES:base + 2 * LANES]
            q_stack = jnp.concatenate(
                [qa, pltpu.roll(qa, HEAD_DIM, 1), qb, pltpu.roll(qb, HEAD_DIM, 1)], axis=0).astype(BF16)
            s = lax.dot_general(q_stack, k_lo[keys], (((1,), (1,)), ((), ())),
                                preferred_element_type=F32)
            probs, sink_w = [], []
            for g in range(GQA_GROUP):
                h = GQA_GROUP * j + g
                bias = bias_ref[h] + no_prev if sb == 0 else bias_ref[h]
                sg = s[blk * g:blk * (g + 1)] + bias
                sink = sinks_ref[h]
                m = jnp.maximum(jnp.max(sg, axis=-1, keepdims=True), sink)
                probs.append(jnp.exp(sg - m).astype(BF16))
                sink_w.append(jnp.exp(sink - m))
            pp = jnp.concatenate(
                [jnp.concatenate([probs[2 * pair], probs[2 * pair + 1]], axis=1)
                 for pair in range(GQA_GROUP // 2)], axis=0)
            o_all = jnp.dot(pp, v_stack, preferred_element_type=F32)
            for pair in range(GQA_GROUP // 2):
                o = o_all[blk * pair:blk * (pair + 1)]
                denom = o[:, LANES:] + jnp.where(lo, sink_w[2 * pair], sink_w[2 * pair + 1])
                o_ref[rows, base + LANES * pair:base + LANES * (pair + 1)] = (
                    o[:, :LANES] / denom).astype(o_ref.dtype)


def _alibi_bias(n_heads):
    blk = ATTN_BLOCK
    slopes = 2.0 ** (-8.0 * jnp.arange(1, n_heads + 1, dtype=F32) / n_heads)
    dist = (jnp.arange(blk) + blk)[:, None] - jnp.arange(2 * blk)[None, :]
    valid = (dist >= 0) & (dist < blk)
    bias = -(slopes[:, None, None] * dist.astype(F32)[None])
    return jnp.where(valid[None], bias, -jnp.inf)


def _attention(y, sinks, bias, batch, seq, d_attn, d_kv):
    t = y.shape[0]
    blk = ATTN_BLOCK
    step = ATTN_STEP_BLOCKS * blk
    ns = seq // step
    kv_w = 2 * d_kv
    kv_col = d_attn // kv_w
    kern = functools.partial(_attn_kernel, n_kv=d_kv // HEAD_DIM)
    grid_spec = pltpu.PrefetchScalarGridSpec(
        num_scalar_prefetch=1,
        grid=(batch, ns),
        in_specs=[
            pl.BlockSpec((step, d_attn), lambda b, i, s: (b * ns + i, 0)),
            pl.BlockSpec((blk, kv_w),
                         lambda b, i, s: ((b * ns + i) * ATTN_STEP_BLOCKS - jnp.minimum(i, 1), kv_col)),
            pl.BlockSpec((step, kv_w), lambda b, i, s: (b * ns + i, kv_col)),
            pl.BlockSpec(bias.shape, lambda b, i, s: (0, 0, 0)),
        ],
        out_specs=pl.BlockSpec((step, d_attn), lambda b, i, s: (b * ns + i, 0)),
    )
    return pl.pallas_call(
        kern,
        out_shape=jax.ShapeDtypeStruct((t, d_attn), BF16),
        grid_spec=grid_spec,
        compiler_params=_params("arbitrary", "arbitrary"),
        name="swa_attention",
    )(sinks, y, y, y, bias)


HALO = BF16_ROWS


def _branch_kernel(attn_ref, bc_ref, cx_ref, hbc_ref, hcx_ref, cw_ref, wa_ref, wc_ref, ga_ref, gc_ref,
                   o_ref, u_scr, conv_scr, *, blocks_per_seq, taps):
    m = pl.program_id(0)
    n = pl.program_id(1)
    dc = conv_scr.shape[1]
    half = dc // 2
    chunk = 64

    @pl.when(n == 0)
    def _():
        keep = jnp.where(m % blocks_per_seq == 0, 0.0, 1.0).astype(F32)
        hc = jnp.concatenate([hbc_ref[:, dc:], hcx_ref[:, :half]], axis=1).astype(F32)
        hx = hcx_ref[:, half:].astype(F32)
        u_scr[0:HALO, :] = hc * hx * keep

        def fill_u(c, carry):
            r0 = pl.multiple_of(c * chunk, chunk)
            cc = jnp.concatenate([bc_ref[pl.ds(r0, chunk), dc:], cx_ref[pl.ds(r0, chunk), :half]], axis=1)
            xx = cx_ref[pl.ds(r0, chunk), half:]
            u_scr[pl.ds(HALO + r0, chunk), :] = cc.astype(F32) * xx.astype(F32)
            return carry

        lax.fori_loop(0, conv_scr.shape[0] // chunk, fill_u, 0)

        def conv_rows(c, carry):
            r0 = pl.multiple_of(c * chunk, chunk)
            w0 = pl.multiple_of(r0 + HALO - 8, 8)
            win = u_scr[pl.ds(w0, chunk + 8), :]
            acc = cw_ref[taps - 1:taps, :] * win[8:]
            for back in range(1, taps):
                acc = acc + cw_ref[taps - 1 - back:taps - back, :] * pltpu.roll(win, back, 0)[8:]
            b = bc_ref[pl.ds(r0, chunk), :dc].astype(F32)
            conv_scr[pl.ds(r0, chunk), :] = (b * acc).astype(conv_scr.dtype)
            return carry

        lax.fori_loop(0, conv_scr.shape[0] // chunk, conv_rows, 0)

    a = jnp.dot(attn_ref[...], wa_ref[...].astype(BF16), preferred_element_type=F32)
    c = jnp.dot(conv_scr[...], wc_ref[...].astype(BF16), preferred_element_type=F32)
    merged = _sigmoid(ga_ref[...].astype(F32)) * a + _sigmoid(gc_ref[...].astype(F32)) * c
    o_ref[...] = merged.astype(o_ref.dtype)


def _branch_merge(attn, y, conv_w, w_attn, w_conv, layer, seq, col_b, col_ga, col_gc):
    t, d_attn = attn.shape
    dc = w_conv.shape[1]
    d = w_attn.shape[-1]
    taps = conv_w.shape[1]
    bm, bn = ROW_BLOCK, 512
    wide = dc + dc // 2
    assert col_b % wide == 0 and taps - 1 <= 8
    cb = col_b // wide
    halo_blocks = bm // HALO
    kern = functools.partial(_branch_kernel, blocks_per_seq=seq // bm, taps=taps)

    def halo_map(col):
        return lambda m, j: (jnp.maximum(m * halo_blocks - 1, 0), col)

    return pl.pallas_call(
        kern,
        out_shape=jax.ShapeDtypeStruct((t, d), BF16),
        grid=(t // bm, d // bn),
        in_specs=[
            pl.BlockSpec((bm, d_attn), lambda m, j: (m, 0)),
            pl.BlockSpec((bm, wide), lambda m, j: (m, cb)),
            pl.BlockSpec((bm, wide), lambda m, j: (m, cb + 1)),
            pl.BlockSpec((HALO, wide), halo_map(cb)),
            pl.BlockSpec((HALO, wide), halo_map(cb + 1)),
            pl.BlockSpec((None, taps, dc), lambda m, j: (layer, 0, 0)),
            pl.BlockSpec((None, d_attn, bn), lambda m, j: (layer, 0, j)),
            pl.BlockSpec((None, dc, bn), lambda m, j: (layer, 0, j)),
            pl.BlockSpec((bm, bn), lambda m, j: (m, col_ga // bn + j)),
            pl.BlockSpec((bm, bn), lambda m, j: (m, col_gc // bn + j)),
        ],
        out_specs=pl.BlockSpec((bm, bn), lambda m, j: (m, j)),
        scratch_shapes=[pltpu.VMEM((bm + HALO, dc), F32), pltpu.VMEM((bm, dc), BF16)],
        compiler_params=_params("arbitrary", "arbitrary"),
        name="branch_merge",
    )(attn, y, y, y, y, conv_w, w_attn, w_conv, y, y)


def _resid_proj_kernel(lhs_ref, w_ref, x_ref, gate_ref, o_ref):
    acc = jnp.dot(lhs_ref[...], w_ref[...].astype(BF16), preferred_element_type=F32)
    o_ref[...] = x_ref[...] + gate_ref[...] * acc


def _resid_proj(lhs, w, layer, x2, gate, seq, bm, bn, name):
    t, k = lhs.shape
    d = w.shape[-1]
    per_seq = seq // bm
    return pl.pallas_call(
        _resid_proj_kernel,
        out_shape=jax.ShapeDtypeStruct((t, d), F32),
        grid=(t // bm, d // bn),
        in_specs=[
            pl.BlockSpec((bm, k), lambda m, j: (m, 0)),
            pl.BlockSpec((None, k, bn), lambda m, j: (layer, 0, j)),
            pl.BlockSpec((bm, bn), lambda m, j: (m, j)),
            pl.BlockSpec((None, 1, bn), lambda m, j: (m // per_seq, 0, j)),
        ],
        out_specs=pl.BlockSpec((bm, bn), lambda m, j: (m, j)),
        compiler_params=_params("arbitrary", "arbitrary"),
        name=name,
    )(lhs, w, x2, gate)


def _ffn_up_kernel(x_ref, gain_ref, scale_ref, shift_ref, wg_ref, wu_ref, o_ref, h_scr):
    @pl.when(pl.program_id(1) == 0)
    def _():
        _fill_modulated_norm(x_ref, gain_ref, scale_ref, shift_ref, h_scr)

    _swiglu_tile(h_scr[...], wg_ref, wu_ref, o_ref, o_ref.shape[0])


def _ffn_up(x2, norm_gain, scale, shift, w_gate_up, layer, li, seq):
    t, d = x2.shape
    f = w_gate_up.shape[-1] // 2
    bm, bn = ROW_BLOCK, 512
    per_seq = seq // bm
    return pl.pallas_call(
        _ffn_up_kernel,
        out_shape=jax.ShapeDtypeStruct((t, f), BF16),
        grid=(t // bm, f // bn),
        in_specs=[
            pl.BlockSpec((bm, d), lambda m, j: (m, 0)),
            pl.BlockSpec((None, 1, d), lambda m, j: (layer, 0, 0)),
            pl.BlockSpec((None, 1, d), lambda m, j: (m // per_seq, 0, 0)),
            pl.BlockSpec((None, 1, d), lambda m, j: (m // per_seq, 0, 0)),
            pl.BlockSpec((None, d, bn), lambda m, j: (li, 0, j)),
            pl.BlockSpec((None, d, bn), lambda m, j: (li, 0, f // bn + j)),
        ],
        out_specs=pl.BlockSpec((bm, bn), lambda m, j: (m, j)),
        scratch_shapes=[pltpu.VMEM((bm, d), BF16)],
        compiler_params=_params("arbitrary", "arbitrary"),
        name="ffn_gate_up",
    )(x2, norm_gain, scale, shift, w_gate_up, w_gate_up)


def _router_kernel(x_ref, gain_ref, scale_ref, shift_ref, rw_ref, h_ref, meta_ref, cnt_ref,
                   tri_scr, base_scr, hl_scr, *, n_experts):
    i = pl.program_id(0)
    rows = x_ref.shape[0]
    chunk = 64

    @pl.when(i == 0)
    def _():
        r = lax.broadcasted_iota(jnp.int32, (rows, rows), 0)
        c = lax.broadcasted_iota(jnp.int32, (rows, rows), 1)
        tri_scr[...] = jnp.where(c < r, 1.0, 0.0).astype(BF16)
        base_scr[...] = jnp.zeros(base_scr.shape, F32)

    gain_scale = gain_ref[...] * (1.0 + scale_ref[...])
    shift = shift_ref[...]

    def body(c, carry):
        r0 = pl.multiple_of(c * chunk, chunk)
        h = _modulated_norm(x_ref[pl.ds(r0, chunk), :], gain_scale, shift)
        h_ref[pl.ds(r0, chunk), :] = h
        hi = h.astype(BF16)
        hl_scr[pl.ds(r0, chunk), :] = hi
        hl_scr[pl.ds(pl.multiple_of(rows + r0, chunk), chunk), :] = (h - hi.astype(F32)).astype(BF16)
        return carry

    lax.fori_loop(0, rows // chunk, body, 0)

    prod = jnp.dot(hl_scr[...], rw_ref[...], preferred_element_type=F32)
    lg = (prod[:rows, :LANES] + prod[:rows, LANES:]) + (prod[rows:, :LANES] + prod[rows:, LANES:])
    lane_i = lax.broadcasted_iota(jnp.int32, lg.shape, 1)
    lg = jnp.where(lane_i < n_experts, lg, -jnp.inf)
    lane = lane_i.astype(F32)
    m1 = jnp.max(lg, axis=-1, keepdims=True)
    i1 = jnp.min(jnp.where(lg == m1, lane, float(LANES)), axis=-1, keepdims=True)
    lg2 = jnp.where(lane == i1, -jnp.inf, lg)
    m2 = jnp.max(lg2, axis=-1, keepdims=True)
    i2 = jnp.min(jnp.where(lg2 == m2, lane, float(LANES)), axis=-1, keepdims=True)
    ex = jnp.exp(m2 - m1)
    w1 = 1.0 / (1.0 + ex)
    w2 = ex / (1.0 + ex)
    onehot = jnp.where(jnp.logical_or(lane == i1, lane == i2), 1.0, 0.0)
    ranks = jnp.dot(tri_scr[...], onehot.astype(BF16), preferred_element_type=F32) + base_scr[...]
    r1 = jnp.sum(jnp.where(lane == i1, ranks, 0.0), axis=-1, keepdims=True)
    r2 = jnp.sum(jnp.where(lane == i2, ranks, 0.0), axis=-1, keepdims=True)
    meta = jnp.zeros(lg.shape, F32)
    for pos, val in enumerate((i1, i2, r1, r2, w1, w2)):
        meta = jnp.where(lane == float(pos), val, meta)
    meta_ref[...] = meta
    new_base = base_scr[...] + jnp.sum(onehot, axis=0, keepdims=True)
    base_scr[...] = new_base
    cnt_ref[...] = new_base


def _router(x2, norm_gain, scale, shift, w_router, layer, seq):
    t, d = x2.shape
    e = w_router.shape[-1]
    bm = ROUTE_BLOCK
    per_seq = seq // bm
    kern = functools.partial(_router_kernel, n_experts=e)
    w_pad = jnp.pad(w_router, ((0, 0), (0, LANES - e)))
    w_hi = w_pad.astype(BF16)
    w_lo = (w_pad - w_hi.astype(F32)).astype(BF16)
    w_hl = jnp.concatenate([w_hi, w_lo], axis=1)
    return pl.pallas_call(
        kern,
        out_shape=(jax.ShapeDtypeStruct((t, d), F32),
                   jax.ShapeDtypeStruct((t, LANES), F32),
                   jax.ShapeDtypeStruct((1, LANES), F32)),
        grid=(t // bm,),
        in_specs=[
            pl.BlockSpec((bm, d), lambda m: (m, 0)),
            pl.BlockSpec((None, 1, d), lambda m: (layer, 0, 0)),
            pl.BlockSpec((None, 1, d), lambda m: (m // per_seq, 0, 0)),
            pl.BlockSpec((None, 1, d), lambda m: (m // per_seq, 0, 0)),
            pl.BlockSpec((d, 2 * LANES), lambda m: (0, 0)),
        ],
        out_specs=(pl.BlockSpec((bm, d), lambda m: (m, 0)),
                   pl.BlockSpec((bm, LANES), lambda m: (m, 0)),
                   pl.BlockSpec((1, LANES), lambda m: (0, 0))),
        scratch_shapes=[pltpu.VMEM((bm, bm), BF16), pltpu.VMEM((1, LANES), F32),
                        pltpu.VMEM((2 * bm, d), BF16)],
        compiler_params=_params("arbitrary"),
        name="moe_router",
    )(x2, norm_gain, scale, shift, w_hl)


def _dispatch_kernel(zend_ref, znum_ref, s1_ref, s2_ref, h_ref, xs_ref, zero_scr, sem, zsem):
    rows = h_ref.shape[0]

    @pl.when(pl.program_id(0) == 0)
    def _():
        zero_scr[...] = jnp.zeros(zero_scr.shape, zero_scr.dtype)

        for e in range(zend_ref.shape[0]):
            def fill(k, e=e):
                start = pl.multiple_of(zend_ref[e] - (k + 1) * ZERO_ROWS, ZERO_ROWS)
                return pltpu.make_async_copy(zero_scr, xs_ref.at[pl.ds(start, ZERO_ROWS)], zsem)

            def start_fill(k, carry, fill=fill):
                fill(k).start()
                return carry

            def wait_fill(k, carry, fill=fill):
                fill(k).wait()
                return carry

            lax.fori_loop(0, znum_ref[e], start_fill, 0)
            lax.fori_loop(0, znum_ref[e], wait_fill, 0)

    def row_copy(i, slot_ref):
        return pltpu.make_async_copy(h_ref.at[pl.ds(i, 1)], xs_ref.at[pl.ds(slot_ref[0, i], 1)], sem)

    def body(i, carry):
        row_copy(i, s1_ref).start(priority=0)
        row_copy(i, s2_ref).start(priority=1)
        return carry

    lax.fori_loop(0, rows, body, 0, unroll=8)
    for _ in range(TOP_K):
        pltpu.make_async_copy(h_ref, xs_ref.at[pl.ds(0, rows)], sem).wait()


def _dispatch(h, slot1, slot2, zero_end, zero_chunks, n_slots):
    t, d = h.shape
    bm = ROW_BLOCK
    nblk = t // bm
    slots = [s.reshape(nblk, 1, bm) for s in (slot1, slot2)]
    smem_spec = pl.BlockSpec((None, 1, bm), lambda m, zs, zn: (m, 0, 0), memory_space=pltpu.SMEM)
    grid_spec = pltpu.PrefetchScalarGridSpec(
        num_scalar_prefetch=2,
        grid=(nblk,),
        in_specs=[smem_spec, smem_spec, pl.BlockSpec((bm, d), lambda m, zs, zn: (m, 0))],
        out_specs=pl.BlockSpec(memory_space=pl.ANY),
        scratch_shapes=[pltpu.VMEM((ZERO_ROWS, d), F32), pltpu.SemaphoreType.DMA(()),
                        pltpu.SemaphoreType.DMA(())],
    )
    return pl.pallas_call(
        _dispatch_kernel,
        out_shape=jax.ShapeDtypeStruct((n_slots, d), F32),
        grid_spec=grid_spec,
        compiler_params=_params("arbitrary"),
        name="moe_dispatch",
    )(zero_end, zero_chunks, slots[0], slots[1], h)


def _filled_quarters(rows_ref):
    shift = MOE_QUARTER.bit_length() - 1
    return lax.shift_right_logical(rows_ref[pl.program_id(0)] + (MOE_QUARTER - 1), shift)


def _moe_up_kernel(be_ref, nl_ref, rows_ref, xs_ref, wg_ref, wu_ref, o_ref, h_scr):
    del be_ref, nl_ref
    quarters = _filled_quarters(rows_ref)
    chunk = MOE_CHUNK

    @pl.when(jnp.logical_and(quarters > 0, pl.program_id(1) == 0))
    def _():
        def body(c, carry):
            r0 = pl.multiple_of(c * chunk, chunk)
            h_scr[pl.ds(r0, chunk), :] = xs_ref[pl.ds(r0, chunk), :].astype(h_scr.dtype)
            return carry

        lax.fori_loop(0, quarters * (MOE_QUARTER // chunk), body, 0)

    for q in range(MOE_BLOCK // MOE_QUARTER + 1):
        rows = q * MOE_QUARTER

        @pl.when(quarters == q)
        def _(rows=rows):
            if rows:
                _swiglu_tile(h_scr[0:rows, :], wg_ref, wu_ref, o_ref, rows)
            if rows < MOE_BLOCK:
                o_ref[rows:, :] = jnp.zeros((MOE_BLOCK - rows, o_ref.shape[1]), o_ref.dtype)


def _moe_up(xs, w_gate_up, li, blk_expert, n_live, blk_rows):
    d = xs.shape[1]
    n_blocks = blk_expert.shape[0]
    f = w_gate_up.shape[-1] // 2
    bm, bn = MOE_BLOCK, 512
    nt = f // bn

    def row(b, nl):
        return jnp.minimum(b, nl[0] - 1)

    def col(b, j, nl):
        return jnp.where(b < nl[0], j, nt - 1)

    grid_spec = pltpu.PrefetchScalarGridSpec(
        num_scalar_prefetch=3,
        grid=(n_blocks, nt),
        in_specs=[
            pl.BlockSpec((bm, d), lambda b, j, be, nl, br: (row(b, nl), 0)),
            pl.BlockSpec((None, None, d, bn),
                         lambda b, j, be, nl, br: (li, be[row(b, nl)], 0, col(b, j, nl))),
            pl.BlockSpec((None, None, d, bn),
                         lambda b, j, be, nl, br: (li, be[row(b, nl)], 0, nt + col(b, j, nl))),
        ],
        out_specs=pl.BlockSpec((bm, bn), lambda b, j, be, nl, br: (b, j)),
        scratch_shapes=[pltpu.VMEM((bm, d), BF16)],
    )
    return pl.pallas_call(
        _moe_up_kernel,
        out_shape=jax.ShapeDtypeStruct((n_blocks * bm, f), BF16),
        grid_spec=grid_spec,
        compiler_params=_params("arbitrary", "arbitrary"),
        name="moe_gate_up",
    )(blk_expert, n_live, blk_rows, xs, w_gate_up, w_gate_up)


def _moe_down_kernel(be_ref, nl_ref, rows_ref, a_ref, w_ref, o_ref):
    del be_ref, nl_ref
    quarters = _filled_quarters(rows_ref)

    for q in range(MOE_BLOCK // MOE_QUARTER + 1):
        rows = q * MOE_QUARTER

        @pl.when(quarters == q)
        def _(rows=rows):
            if rows:
                o_ref[0:rows, :] = jnp.dot(a_ref[0:rows, :], w_ref[...].astype(BF16),
                                           preferred_element_type=F32)
            if rows < MOE_BLOCK:
                o_ref[rows:, :] = jnp.zeros((MOE_BLOCK - rows, o_ref.shape[1]), o_ref.dtype)


def _moe_down(a, w_down, li, blk_expert, n_live, blk_rows):
    s, f = a.shape
    d = w_down.shape[-1]
    bm, bn = MOE_BLOCK, 256
    nt = d // bn

    def row(b, nl):
        return jnp.minimum(b, nl[0] - 1)

    def col(b, j, nl):
        return jnp.where(b < nl[0], j, nt - 1)

    grid_spec = pltpu.PrefetchScalarGridSpec(
        num_scalar_prefetch=3,
        grid=(s // bm, nt),
        in_specs=[
            pl.BlockSpec((bm, f), lambda b, j, be, nl, br: (row(b, nl), 0)),
            pl.BlockSpec((None, None, f, bn),
                         lambda b, j, be, nl, br: (li, be[row(b, nl)], 0, col(b, j, nl))),
        ],
        out_specs=pl.BlockSpec((bm, bn), lambda b, j, be, nl, br: (b, j)),
    )
    return pl.pallas_call(
        _moe_down_kernel,
        out_shape=jax.ShapeDtypeStruct((s, d), F32),
        grid_spec=grid_spec,
        compiler_params=_params("arbitrary", "arbitrary"),
        name="moe_down",
    )(blk_expert, n_live, blk_rows, a, w_down)


def _combine_kernel(s1_ref, s2_ref, n1_ref, n2_ref, x_ref, gate_ref, meta_ref, ys_ref, o_ref, gbuf, sem):
    m = pl.program_id(0)
    rows = x_ref.shape[0]
    chunk = 128

    def issue(par, first_ref, second_ref):
        def row_copy(i, slot_ref, k):
            return pltpu.make_async_copy(ys_ref.at[pl.ds(slot_ref[0, i], 1)],
                                         gbuf.at[par, k, pl.ds(i, 1)], sem.at[par, k])

        def body(g, carry):
            r0 = pl.multiple_of(g * 8, 8)
            for k in range(8):
                row_copy(r0 + k, first_ref, 0).start(priority=0)
                row_copy(r0 + k, second_ref, 1).start(priority=1)
            return carry

        lax.fori_loop(0, rows // 8, body, 0)

    def step(par):
        @pl.when(m == 0)
        def _():
            issue(par, s1_ref, s2_ref)

        @pl.when(m + 1 < pl.num_programs(0))
        def _():
            issue(1 - par, n1_ref, n2_ref)

        for k in range(TOP_K):
            pltpu.make_async_copy(ys_ref.at[pl.ds(0, rows)], gbuf.at[par, k], sem.at[par, k]).wait()

        gate = gate_ref[...]

        def mix(c, carry):
            sl = pl.ds(pl.multiple_of(c * chunk, chunk), chunk)
            w1 = meta_ref[sl, 4:5]
            w2 = meta_ref[sl, 5:6]
            o_ref[sl, :] = x_ref[sl, :] + gate * (w1 * gbuf[par, 0, sl, :] + w2 * gbuf[par, 1, sl, :])
            return carry

        lax.fori_loop(0, rows // chunk, mix, 0)

    for par in range(2):
        pl.when(m % 2 == par)(functools.partial(step, par))


def _combine(ys, slot1, slot2, meta, x2, gate, seq):
    t, d = x2.shape
    bm = ROUTE_BLOCK
    nblk = t // bm
    per_seq = seq // bm
    slots = [s.reshape(nblk, 1, bm) for s in (slot1, slot2)]
    smem_spec = pl.BlockSpec((None, 1, bm), lambda m: (m, 0, 0), memory_space=pltpu.SMEM)
    next_spec = pl.BlockSpec((None, 1, bm), lambda m: (jnp.minimum(m + 1, nblk - 1), 0, 0),
                             memory_space=pltpu.SMEM)
    return pl.pallas_call(
        _combine_kernel,
        out_shape=jax.ShapeDtypeStruct((t, d), F32),
        grid=(nblk,),
        in_specs=[smem_spec, smem_spec, next_spec, next_spec,
                  pl.BlockSpec((bm, d), lambda m: (m, 0)),
                  pl.BlockSpec((None, 1, d), lambda m: (m // per_seq, 0, 0)),
                  pl.BlockSpec((bm, LANES), lambda m: (m, 0)),
                  pl.BlockSpec(memory_space=pl.ANY)],
        out_specs=pl.BlockSpec((bm, d), lambda m: (m, 0)),
        scratch_shapes=[pltpu.VMEM((2, TOP_K, bm, d), F32), pltpu.SemaphoreType.DMA((2, TOP_K))],
        compiler_params=_params("arbitrary"),
        name="moe_combine",
    )(slots[0], slots[1], slots[0], slots[1], x2, gate, meta, ys)


def _moe(x2, norm_gain, scale, shift, gate, w_router, w_gate_up, w_down, layer, li, seq):
    t, d = x2.shape
    e = w_router.shape[-1]
    blk = MOE_BLOCK
    h, meta, counts = _router(x2, norm_gain, scale, shift, w_router[li], layer, seq)

    cnt = counts[0, :e].astype(jnp.int32)
    nblk = (cnt + blk - 1) // blk
    blk_end = jnp.cumsum(nblk)
    blk_start = blk_end - nblk
    off = blk_start * blk
    n_blocks = pl.cdiv(TOP_K * t, blk) + e
    n_live = blk_end[-1:].astype(jnp.int32)
    block_ids = jnp.arange(n_blocks, dtype=jnp.int32)
    blk_expert = jnp.minimum(jnp.sum(block_ids[:, None] >= blk_end[None, :], axis=1), e - 1).astype(jnp.int32)
    blk_rows = jnp.clip(cnt[blk_expert] - (block_ids - blk_start[blk_expert]) * blk, 0, blk)
    blk_rows = jnp.where(block_ids < n_live[0], blk_rows, 0).astype(jnp.int32)
    ids = meta[:, 0:2].astype(jnp.int32)
    ranks = meta[:, 2:4].astype(jnp.int32)
    slots = off[ids] + ranks
    slot1, slot2 = slots[:, 0], slots[:, 1]

    n_slots = n_blocks * blk
    zero_end = jnp.concatenate([off[1:], jnp.full((1,), n_slots, jnp.int32)]).astype(jnp.int32)
    zero_chunks = ((zero_end - (off + cnt) + ZERO_ROWS - 1) // ZERO_ROWS).astype(jnp.int32)
    xs = _dispatch(h, slot1, slot2, zero_end, zero_chunks, n_slots)
    a = _moe_up(xs, w_gate_up, li, blk_expert, n_live, blk_rows)
    ys = _moe_down(a, w_down, li, blk_expert, n_live, blk_rows)
    return _combine(ys, slot1, slot2, meta, x2, gate, seq)


def kernel(x, c, ada_w, ada_b, norm_mix, w_in, q_norm, k_norm, attn_sinks, conv_w, w_attn_branch,
           w_conv_branch, w_out, norm_ffn, ffn_w_gate_up, ffn_w_down, moe_w_router, moe_w_gate_up,
           moe_w_down):
    batch, seq, d = x.shape
    depth = ada_w.shape[0]
    d_attn = w_attn_branch.shape[1]
    d_conv = w_conv_branch.shape[1]
    n_heads = d_attn // HEAD_DIM
    d_kv = (n_heads // GQA_GROUP) * HEAD_DIM
    col_b = d_attn + 2 * d_kv
    col_ga = col_b + 3 * d_conv
    col_gc = col_ga + d
    assert w_in.shape[-1] == col_gc + d and seq % ROW_BLOCK == 0
    assert 2 * HEAD_DIM == LANES and GQA_GROUP == 4 and (n_heads // GQA_GROUP) % 2 == 0

    n_mod = ada_w.shape[-1] // d
    mod = _adaln(c, ada_w, ada_b).reshape(depth, batch, n_mod, 1, d)
    bias = _alibi_bias(n_heads)
    x2 = x.reshape(batch * seq, d)

    for l in range(depth):
        shift_m, scale_m, gate_m, shift_f, scale_f, gate_f = (mod[l, :, i] for i in range(n_mod))

        reps_q, reps_k = d_attn // HEAD_DIM, d_kv // HEAD_DIM
        head_gain = jnp.concatenate([jnp.tile(q_norm[l] * HEAD_DIM ** -0.5, reps_q),
                                     jnp.tile(k_norm[l], reps_k), jnp.ones((d_kv,), F32)])[None]
        head_mask = jnp.concatenate([jnp.ones((d_attn + d_kv,), F32), jnp.zeros((d_kv,), F32)])[None]

        h_mix = _modnorm(x2, norm_mix.reshape(depth, 1, d), scale_m, shift_m, l, seq)
        y = _inproj(h_mix, w_in, l, head_gain, head_mask)
        attn = _attention(y, attn_sinks[l], bias, batch, seq, d_attn, d_kv)
        merged = _branch_merge(attn, y, conv_w, w_attn_branch, w_conv_branch, l, seq, col_b, col_ga, col_gc)
        x2 = _resid_proj(merged, w_out, l, x2, gate_m, seq, 2 * ROW_BLOCK, 512, "mixer_out_proj")

        norm_f = norm_ffn.reshape(depth, 1, d)
        if l % 2 == 0:
            a = _ffn_up(x2, norm_f, scale_f, shift_f, ffn_w_gate_up, l, l // 2, seq)
            x2 = _resid_proj(a, _to_bf16(ffn_w_down, 4, 8), l // 2, x2, gate_f, seq, ROW_BLOCK, 512, "ffn_down")
        else:
            x2 = _moe(x2, norm_f, scale_f, shift_f, gate_f, moe_w_router, moe_w_gate_up, moe_w_down,
                      l, l // 2, seq)
    return x2.reshape(batch, seq, d)
```

```python
import functools

import jax
import jax.numpy as jnp
from jax import lax
from jax.experimental import pallas as pl
from jax.experimental.pallas import tpu as pltpu

F32 = jnp.float32
BF16 = jnp.bfloat16

HEAD_DIM = 64
GQA_GROUP = 4
ATTN_BLOCK = 128
TOP_K = 2
EPS = 1e-6

LANES = 128
BF16_ROWS = 16
V7X_VMEM_BYTES = 64 * 1024 * 1024
VMEM_LIMIT = V7X_VMEM_BYTES - 8 * 1024 * 1024

ROW_BLOCK = 1024
ROUTE_BLOCK = 512
MOE_BLOCK = ROW_BLOCK
MOE_QUARTER = MOE_BLOCK // 4
MOE_CHUNK = 128
ZERO_ROWS = 256


def _params(*sem):
    return pltpu.CompilerParams(dimension_semantics=sem, vmem_limit_bytes=VMEM_LIMIT)


def _sigmoid(x):
    return 0.5 + 0.5 * jnp.tanh(0.5 * x)


def _silu(x):
    half = 0.5 * x
    return half + half * jnp.tanh(half)


MXU_COLS = 256


def _swiglu_tile(h, wg_ref, wu_ref, o_ref, rows):
    for c0 in range(0, o_ref.shape[1], MXU_COLS):
        cols = slice(c0, c0 + MXU_COLS)
        g = jnp.dot(h, wg_ref[:, cols].astype(BF16), preferred_element_type=F32)
        u = jnp.dot(h, wu_ref[:, cols].astype(BF16), preferred_element_type=F32)
        o_ref[0:rows, cols] = (_silu(g) * u).astype(o_ref.dtype)


def _modulated_norm(x, gain_scale, shift):
    ms = jnp.mean(x * x, axis=-1, keepdims=True)
    return (x * lax.rsqrt(ms + EPS)) * gain_scale + shift


def _fill_modulated_norm(x_ref, gain_ref, scale_ref, shift_ref, h_ref, chunk=128):
    gain_scale = gain_ref[...] * (1.0 + scale_ref[...])
    shift = shift_ref[...]

    def body(c, carry):
        r0 = pl.multiple_of(c * chunk, chunk)
        h = _modulated_norm(x_ref[pl.ds(r0, chunk), :], gain_scale, shift)
        h_ref[pl.ds(r0, chunk), :] = h.astype(h_ref.dtype)
        return carry

    lax.fori_loop(0, x_ref.shape[0] // chunk, body, 0)


def _cast_kernel(x_ref, o_ref, *, chunk):
    def body(c, carry):
        sl = pl.ds(pl.multiple_of(c * chunk, chunk), chunk)
        o_ref[sl, :] = x_ref[sl, :].astype(o_ref.dtype)
        return carry

    lax.fori_loop(0, x_ref.shape[0] // chunk, body, 0)


def _to_bf16(w, row_blocks, chunks):
    n, r, c = w.shape
    br = r // row_blocks
    assert br * row_blocks == r and br % (chunks * BF16_ROWS) == 0
    spec = pl.BlockSpec((None, br, c), lambda i, a: (i, a, 0))
    return pl.pallas_call(
        functools.partial(_cast_kernel, chunk=br // chunks),
        out_shape=jax.ShapeDtypeStruct(w.shape, BF16),
        grid=(n, row_blocks),
        in_specs=[spec],
        out_specs=spec,
        compiler_params=_params("arbitrary", "arbitrary"),
        name="weight_to_bf16",
    )(w)


def _adaln_kernel(c_ref, w_ref, b_ref, o_ref):
    s = _silu(c_ref[...]).astype(BF16)
    o_ref[...] = jnp.dot(s, w_ref[...].astype(BF16), preferred_element_type=F32) + b_ref[...]


def _adaln(c, ada_w, ada_b):
    depth, d, n = ada_w.shape
    b = c.shape[0]
    rows = 8 * pl.cdiv(b, 8)
    c_pad = jnp.pad(c, ((0, rows - b), (0, 0)))
    bn = 1024
    out = pl.pallas_call(
        _adaln_kernel,
        out_shape=jax.ShapeDtypeStruct((depth, rows, n), F32),
        grid=(depth, n // bn),
        in_specs=[
            pl.BlockSpec((rows, d), lambda l, j: (0, 0)),
            pl.BlockSpec((None, d, bn), lambda l, j: (l, 0, j)),
            pl.BlockSpec((None, 1, bn), lambda l, j: (l, 0, j)),
        ],
        out_specs=pl.BlockSpec((None, rows, bn), lambda l, j: (l, 0, j)),
        compiler_params=_params("arbitrary", "arbitrary"),
        name="adaln_mod",
    )(c_pad, ada_w, ada_b.reshape(depth, 1, n))
    return out[:, :b]


def _modnorm_kernel(x_ref, gain_ref, scale_ref, shift_ref, o_ref):
    _fill_modulated_norm(x_ref, gain_ref, scale_ref, shift_ref, o_ref)


def _modnorm(x2, norm_gain, scale, shift, layer, seq):
    t, d = x2.shape
    bm = ROUTE_BLOCK
    per_seq = seq // bm
    return pl.pallas_call(
        _modnorm_kernel,
        out_shape=jax.ShapeDtypeStruct((t, d), BF16),
        grid=(t // bm,),
        in_specs=[
            pl.BlockSpec((bm, d), lambda m: (m, 0)),
            pl.BlockSpec((None, 1, d), lambda m: (layer, 0, 0)),
            pl.BlockSpec((None, 1, d), lambda m: (m // per_seq, 0, 0)),
            pl.BlockSpec((None, 1, d), lambda m: (m // per_seq, 0, 0)),
        ],
        out_specs=pl.BlockSpec((bm, d), lambda m: (m, 0)),
        compiler_params=_params("arbitrary"),
        name="mixer_modnorm",
    )(x2, norm_gain, scale, shift)


def _inproj_kernel(h_ref, w_ref, hg_ref, hm_ref, o_ref, *, norm_tiles):
    n = pl.program_id(1)
    acc = jnp.dot(h_ref[...], w_ref[...].astype(BF16), preferred_element_type=F32)

    @pl.when(n < norm_tiles)
    def _():
        bn = acc.shape[1]
        shift = HEAD_DIM.bit_length() - 1
        r = lax.shift_right_logical(lax.broadcasted_iota(jnp.int32, (bn, bn), 0), shift)
        c = lax.shift_right_logical(lax.broadcasted_iota(jnp.int32, (bn, bn), 1), shift)
        head_mean = jnp.where(r == c, 1.0 / HEAD_DIM, 0.0).astype(BF16)
        ms = jnp.dot((acc * acc).astype(BF16), head_mean, preferred_element_type=F32)
        normed = acc * lax.rsqrt(ms + EPS) * hg_ref[...]
        o_ref[...] = jnp.where(hm_ref[...] > 0.0, normed, acc).astype(o_ref.dtype)

    @pl.when(n >= norm_tiles)
    def _():
        o_ref[...] = acc.astype(o_ref.dtype)


def _inproj(h, w_in, layer, head_gain, head_mask):
    t, d = h.shape
    n = w_in.shape[-1]
    bm, bn = 2 * ROW_BLOCK, 512
    norm_cols = head_gain.shape[-1]
    kern = functools.partial(_inproj_kernel, norm_tiles=norm_cols // bn)
    return pl.pallas_call(
        kern,
        out_shape=jax.ShapeDtypeStruct((t, n), BF16),
        grid=(t // bm, n // bn),
        in_specs=[
            pl.BlockSpec((bm, d), lambda m, j: (m, 0)),
            pl.BlockSpec((None, d, bn), lambda m, j: (layer, 0, j)),
            pl.BlockSpec((1, bn), lambda m, j: (0, jnp.minimum(j, norm_cols // bn - 1))),
            pl.BlockSpec((1, bn), lambda m, j: (0, jnp.minimum(j, norm_cols // bn - 1))),
        ],
        out_specs=pl.BlockSpec((bm, bn), lambda m, j: (m, j)),
        compiler_params=_params("arbitrary", "arbitrary"),
        name="mixer_in_proj",
    )(h, w_in, head_gain, head_mask)


ATTN_STEP_BLOCKS = 2


def _attn_kernel(sinks_ref, q_ref, kvp_ref, kvc_ref, bias_ref, o_ref, *, n_kv):
    blk = ATTN_BLOCK
    first = pl.program_id(1) == 0
    dkv = n_kv * HEAD_DIM
    q = q_ref[...].astype(F32)
    kv = jnp.concatenate([kvp_ref[...], kvc_ref[...]], axis=0).astype(F32)
    k_all = kv[:, :dkv]
    v_all = kv[:, dkv:]
    lo = lax.broadcasted_iota(jnp.int32, (1, LANES), 1) < HEAD_DIM
    col = lax.broadcasted_iota(jnp.int32, (1, 2 * blk), 1)
    no_prev = jnp.where(jnp.logical_and(first, col < blk), -jnp.inf, 0.0).astype(F32)
    first_head = lax.broadcasted_iota(jnp.int32, (4 * blk, LANES), 0) < 2 * blk
    lo_full = lax.broadcasted_iota(jnp.int32, (4 * blk, LANES), 1) < HEAD_DIM
    sum_cols = jnp.where(first_head == lo_full, 1.0, 0.0).astype(BF16)

    for j in range(n_kv):
        t, half = divmod(j, 2)
        kp = k_all[:, LANES * t:LANES * (t + 1)]
        vp = v_all[:, LANES * t:LANES * (t + 1)]
        kr = pltpu.roll(kp, HEAD_DIM, 1)
        vr = pltpu.roll(vp, HEAD_DIM, 1)
        k_lo = jnp.where(lo, kr if half else kp, 0.0).astype(BF16)
        v_lo = jnp.where(lo, vr if half else vp, 0.0).astype(BF16)
        v_hi = jnp.where(lo, 0.0, vp if half else vr).astype(BF16)
        base = GQA_GROUP * HEAD_DIM * j
        for sb in range(ATTN_STEP_BLOCKS):
            rows = slice(blk * sb, blk * (sb + 1))
            keys = slice(blk * sb, blk * (sb + 2))
            v_stack = jnp.concatenate(
                [jnp.concatenate([v_lo[keys], v_hi[keys]], axis=0), sum_cols], axis=1)
            qa = q[rows, base:base + LANES]
            qb = q[rows, base + LANES:base + 2 * LANES]
            q_stack = jnp.concatenate(
                [qa, pltpu.roll(qa, HEAD_DIM, 1), qb, pltpu.roll(qb, HEAD_DIM, 1)], axis=0).astype(BF16)
            s = lax.dot_general(q_stack, k_lo[keys], (((1,), (1,)), ((), ())),
                                preferred_element_type=F32)
            probs, sink_w = [], []
            for g in range(GQA_GROUP):
                h = GQA_GROUP * j + g
                bias = bias_ref[h] + no_prev if sb == 0 else bias_ref[h]
                sg = s[blk * g:blk * (g + 1)] + bias
                sink = sinks_ref[h]
                m = jnp.maximum(jnp.max(sg, axis=-1, keepdims=True), sink)
                probs.append(jnp.exp(sg - m).astype(BF16))
                sink_w.append(jnp.exp(sink - m))
            pp = jnp.concatenate(
                [jnp.concatenate([probs[2 * pair], probs[2 * pair + 1]], axis=1)
                 for pair in range(GQA_GROUP // 2)], axis=0)
            o_all = jnp.dot(pp, v_stack, preferred_element_type=F32)
            for pair in range(GQA_GROUP // 2):
                o = o_all[blk * pair:blk * (pair + 1)]
                denom = o[:, LANES:] + jnp.where(lo, sink_w[2 * pair], sink_w[2 * pair + 1])
                o_ref[rows, base + LANES * pair:base + LANES * (pair + 1)] = (
                    o[:, :LANES] / denom).astype(o_ref.dtype)


def _alibi_bias(n_heads):
    blk = ATTN_BLOCK
    slopes = 2.0 ** (-8.0 * jnp.arange(1, n_heads + 1, dtype=F32) / n_heads)
    dist = (jnp.arange(blk) + blk)[:, None] - jnp.arange(2 * blk)[None, :]
    valid = (dist >= 0) & (dist < blk)
    bias = -(slopes[:, None, None] * dist.astype(F32)[None])
    return jnp.where(valid[None], bias, -jnp.inf)


def _attention(y, sinks, bias, batch, seq, d_attn, d_kv):
    t = y.shape[0]
    blk = ATTN_BLOCK
    step = ATTN_STEP_BLOCKS * blk
    ns = seq // step
    kv_w = 2 * d_kv
    kv_col = d_attn // kv_w
    kern = functools.partial(_attn_kernel, n_kv=d_kv // HEAD_DIM)
    grid_spec = pltpu.PrefetchScalarGridSpec(
        num_scalar_prefetch=1,
        grid=(batch, ns),
        in_specs=[
            pl.BlockSpec((step, d_attn), lambda b, i, s: (b * ns + i, 0)),
            pl.BlockSpec((blk, kv_w),
                         lambda b, i, s: ((b * ns + i) * ATTN_STEP_BLOCKS - jnp.minimum(i, 1), kv_col)),
            pl.BlockSpec((step, kv_w), lambda b, i, s: (b * ns + i, kv_col)),
            pl.BlockSpec(bias.shape, lambda b, i, s: (0, 0, 0)),
        ],
        out_specs=pl.BlockSpec((step, d_attn), lambda b, i, s: (b * ns + i, 0)),
    )
    return pl.pallas_call(
        kern,
        out_shape=jax.ShapeDtypeStruct((t, d_attn), BF16),
        grid_spec=grid_spec,
        compiler_params=_params("arbitrary", "arbitrary"),
        name="swa_attention",
    )(sinks, y, y, y, bias)


HALO = BF16_ROWS


GATE_COLS = 512


def _branch_kernel(attn_ref, bc_ref, cx_ref, hbc_ref, hcx_ref, cw_ref, wa_ref, wc_ref, *rest,
                   blocks_per_seq, taps):
    n_gate = (len(rest) - 3) // 2
    ga_refs, gc_refs = rest[:n_gate], rest[n_gate:2 * n_gate]
    o_ref, u_scr, conv_scr = rest[2 * n_gate:]
    m = pl.program_id(0)
    n = pl.program_id(1)
    dc = conv_scr.shape[1]
    half = dc // 2
    chunk = 64

    @pl.when(n == 0)
    def _():
        keep = jnp.where(m % blocks_per_seq == 0, 0.0, 1.0).astype(F32)
        hc = jnp.concatenate([hbc_ref[:, dc:], hcx_ref[:, :half]], axis=1).astype(F32)
        hx = hcx_ref[:, half:].astype(F32)
        u_scr[0:HALO, :] = hc * hx * keep

        def fill_u(c, carry):
            r0 = pl.multiple_of(c * chunk, chunk)
            cc = jnp.concatenate([bc_ref[pl.ds(r0, chunk), dc:], cx_ref[pl.ds(r0, chunk), :half]], axis=1)
            xx = cx_ref[pl.ds(r0, chunk), half:]
            u_scr[pl.ds(HALO + r0, chunk), :] = cc.astype(F32) * xx.astype(F32)
            return carry

        lax.fori_loop(0, conv_scr.shape[0] // chunk, fill_u, 0)

        def conv_rows(c, carry):
            r0 = pl.multiple_of(c * chunk, chunk)
            w0 = pl.multiple_of(r0 + HALO - 8, 8)
            win = u_scr[pl.ds(w0, chunk + 8), :]
            acc = cw_ref[taps - 1:taps, :] * win[8:]
            for back in range(1, taps):
                acc = acc + cw_ref[taps - 1 - back:taps - back, :] * pltpu.roll(win, back, 0)[8:]
            b = bc_ref[pl.ds(r0, chunk), :dc].astype(F32)
            conv_scr[pl.ds(r0, chunk), :] = (b * acc).astype(conv_scr.dtype)
            return carry

        lax.fori_loop(0, conv_scr.shape[0] // chunk, conv_rows, 0)

    a = jnp.dot(attn_ref[...], wa_ref[...].astype(BF16), preferred_element_type=F32)
    c = jnp.dot(conv_scr[...], wc_ref[...].astype(BF16), preferred_element_type=F32)
    for i, (ga_ref, gc_ref) in enumerate(zip(ga_refs, gc_refs)):
        cols = slice(i * GATE_COLS, (i + 1) * GATE_COLS)
        merged = (_sigmoid(ga_ref[...].astype(F32)) * a[:, cols]
                  + _sigmoid(gc_ref[...].astype(F32)) * c[:, cols])
        o_ref[:, cols] = merged.astype(o_ref.dtype)


def _branch_merge(attn, y, conv_w, w_attn, w_conv, layer, seq, col_b, col_ga, col_gc):
    t, d_attn = attn.shape
    dc = w_conv.shape[1]
    d = w_attn.shape[-1]
    taps = conv_w.shape[1]
    bm, bn = ROW_BLOCK, 1024
    wide = dc + dc // 2
    assert col_b % wide == 0 and taps - 1 <= 8
    assert col_ga % GATE_COLS == 0 and col_gc % GATE_COLS == 0 and bn % GATE_COLS == 0
    gates = bn // GATE_COLS
    cb = col_b // wide
    halo_blocks = bm // HALO
    kern = functools.partial(_branch_kernel, blocks_per_seq=seq // bm, taps=taps)

    def halo_map(col):
        return lambda m, j: (jnp.maximum(m * halo_blocks - 1, 0), col)

    return pl.pallas_call(
        kern,
        out_shape=jax.ShapeDtypeStruct((t, d), BF16),
        grid=(t // bm, d // bn),
        in_specs=[
            pl.BlockSpec((bm, d_attn), lambda m, j: (m, 0)),
            pl.BlockSpec((bm, wide), lambda m, j: (m, cb)),
            pl.BlockSpec((bm, wide), lambda m, j: (m, cb + 1)),
            pl.BlockSpec((HALO, wide), halo_map(cb)),
            pl.BlockSpec((HALO, wide), halo_map(cb + 1)),
            pl.BlockSpec((None, taps, dc), lambda m, j: (layer, 0, 0)),
            pl.BlockSpec((None, d_attn, bn), lambda m, j: (layer, 0, j)),
            pl.BlockSpec((None, dc, bn), lambda m, j: (layer, 0, j)),
        ] + [
            pl.BlockSpec((bm, GATE_COLS), lambda m, j, c0=c0, i=i: (m, c0 // GATE_COLS + gates * j + i))
            for c0 in (col_ga, col_gc) for i in range(gates)
        ],
        out_specs=pl.BlockSpec((bm, bn), lambda m, j: (m, j)),
        scratch_shapes=[pltpu.VMEM((bm + HALO, dc), F32), pltpu.VMEM((bm, dc), BF16)],
        compiler_params=_params("arbitrary", "arbitrary"),
        name="branch_merge",
    )(attn, y, y, y, y, conv_w, w_attn, w_conv, *([y] * (2 * gates)))


def _resid_proj_kernel(lhs_ref, w_ref, x_ref, gate_ref, o_ref):
    acc = jnp.dot(lhs_ref[...], w_ref[...].astype(BF16), preferred_element_type=F32)
    o_ref[...] = x_ref[...] + gate_ref[...] * acc


def _resid_proj(lhs, w, layer, x2, gate, seq, bm, bn, name):
    t, k = lhs.shape
    d = w.shape[-1]
    per_seq = seq // bm
    return pl.pallas_call(
        _resid_proj_kernel,
        out_shape=jax.ShapeDtypeStruct((t, d), F32),
        grid=(t // bm, d // bn),
        in_specs=[
            pl.BlockSpec((bm, k), lambda m, j: (m, 0)),
            pl.BlockSpec((None, k, bn), lambda m, j: (layer, 0, j)),
            pl.BlockSpec((bm, bn), lambda m, j: (m, j)),
            pl.BlockSpec((None, 1, bn), lambda m, j: (m // per_seq, 0, j)),
        ],
        out_specs=pl.BlockSpec((bm, bn), lambda m, j: (m, j)),
        compiler_params=_params("arbitrary", "arbitrary"),
        name=name,
    )(lhs, w, x2, gate)


def _ffn_up_kernel(x_ref, gain_ref, scale_ref, shift_ref, wg_ref, wu_ref, o_ref, h_scr):
    @pl.when(pl.program_id(1) == 0)
    def _():
        _fill_modulated_norm(x_ref, gain_ref, scale_ref, shift_ref, h_scr)

    _swiglu_tile(h_scr[...], wg_ref, wu_ref, o_ref, o_ref.shape[0])


def _ffn_up(x2, norm_gain, scale, shift, w_gate_up, layer, li, seq):
    t, d = x2.shape
    f = w_gate_up.shape[-1] // 2
    bm, bn = ROW_BLOCK, 512
    per_seq = seq // bm
    return pl.pallas_call(
        _ffn_up_kernel,
        out_shape=jax.ShapeDtypeStruct((t, f), BF16),
        grid=(t // bm, f // bn),
        in_specs=[
            pl.BlockSpec((bm, d), lambda m, j: (m, 0)),
            pl.BlockSpec((None, 1, d), lambda m, j: (layer, 0, 0)),
            pl.BlockSpec((None, 1, d), lambda m, j: (m // per_seq, 0, 0)),
            pl.BlockSpec((None, 1, d), lambda m, j: (m // per_seq, 0, 0)),
            pl.BlockSpec((None, d, bn), lambda m, j: (li, 0, j)),
            pl.BlockSpec((None, d, bn), lambda m, j: (li, 0, f // bn + j)),
        ],
        out_specs=pl.BlockSpec((bm, bn), lambda m, j: (m, j)),
        scratch_shapes=[pltpu.VMEM((bm, d), BF16)],
        compiler_params=_params("arbitrary", "arbitrary"),
        name="ffn_gate_up",
    )(x2, norm_gain, scale, shift, w_gate_up, w_gate_up)


def _router_kernel(x_ref, gain_ref, scale_ref, shift_ref, rw_ref, h_ref, meta_ref, cnt_ref,
                   tri_scr, base_scr, hl_scr, *, n_experts):
    i = pl.program_id(0)
    rows = x_ref.shape[0]
    chunk = 64

    @pl.when(i == 0)
    def _():
        r = lax.broadcasted_iota(jnp.int32, (rows, rows), 0)
        c = lax.broadcasted_iota(jnp.int32, (rows, rows), 1)
        tri_scr[...] = jnp.where(c < r, 1.0, 0.0).astype(BF16)
        base_scr[...] = jnp.zeros(base_scr.shape, F32)

    gain_scale = gain_ref[...] * (1.0 + scale_ref[...])
    shift = shift_ref[...]

    def body(c, carry):
        r0 = pl.multiple_of(c * chunk, chunk)
        h = _modulated_norm(x_ref[pl.ds(r0, chunk), :], gain_scale, shift)
        h_ref[pl.ds(r0, chunk), :] = h
        hi = h.astype(BF16)
        hl_scr[pl.ds(r0, chunk), :] = hi
        hl_scr[pl.ds(pl.multiple_of(rows + r0, chunk), chunk), :] = (h - hi.astype(F32)).astype(BF16)
        return carry

    lax.fori_loop(0, rows // chunk, body, 0)

    prod = jnp.dot(hl_scr[...], rw_ref[...], preferred_element_type=F32)
    lg = (prod[:rows, :LANES] + prod[:rows, LANES:]) + (prod[rows:, :LANES] + prod[rows:, LANES:])
    lane_i = lax.broadcasted_iota(jnp.int32, lg.shape, 1)
    lg = jnp.where(lane_i < n_experts, lg, -jnp.inf)
    lane = lane_i.astype(F32)
    m1 = jnp.max(lg, axis=-1, keepdims=True)
    i1 = jnp.min(jnp.where(lg == m1, lane, float(LANES)), axis=-1, keepdims=True)
    lg2 = jnp.where(lane == i1, -jnp.inf, lg)
    m2 = jnp.max(lg2, axis=-1, keepdims=True)
    i2 = jnp.min(jnp.where(lg2 == m2, lane, float(LANES)), axis=-1, keepdims=True)
    ex = jnp.exp(m2 - m1)
    w1 = 1.0 / (1.0 + ex)
    w2 = ex / (1.0 + ex)
    onehot = jnp.where(jnp.logical_or(lane == i1, lane == i2), 1.0, 0.0)
    ranks = jnp.dot(tri_scr[...], onehot.astype(BF16), preferred_element_type=F32) + base_scr[...]
    r1 = jnp.sum(jnp.where(lane == i1, ranks, 0.0), axis=-1, keepdims=True)
    r2 = jnp.sum(jnp.where(lane == i2, ranks, 0.0), axis=-1, keepdims=True)
    meta = jnp.zeros(lg.shape, F32)
    for pos, val in enumerate((i1, i2, r1, r2, w1, w2)):
        meta = jnp.where(lane == float(pos), val, meta)
    meta_ref[...] = meta
    new_base = base_scr[...] + jnp.sum(onehot, axis=0, keepdims=True)
    base_scr[...] = new_base
    cnt_ref[...] = new_base


def _router(x2, norm_gain, scale, shift, w_router, layer, seq):
    t, d = x2.shape
    e = w_router.shape[-1]
    bm = ROUTE_BLOCK
    per_seq = seq // bm
    kern = functools.partial(_router_kernel, n_experts=e)
    w_pad = jnp.pad(w_router, ((0, 0), (0, LANES - e)))
    w_hi = w_pad.astype(BF16)
    w_lo = (w_pad - w_hi.astype(F32)).astype(BF16)
    w_hl = jnp.concatenate([w_hi, w_lo], axis=1)
    return pl.pallas_call(
        kern,
        out_shape=(jax.ShapeDtypeStruct((t, d), F32),
                   jax.ShapeDtypeStruct((t, LANES), F32),
                   jax.ShapeDtypeStruct((1, LANES), F32)),
        grid=(t // bm,),
        in_specs=[
            pl.BlockSpec((bm, d), lambda m: (m, 0)),
            pl.BlockSpec((None, 1, d), lambda m: (layer, 0, 0)),
            pl.BlockSpec((None, 1, d), lambda m: (m // per_seq, 0, 0)),
            pl.BlockSpec((None, 1, d), lambda m: (m // per_seq, 0, 0)),
            pl.BlockSpec((d, 2 * LANES), lambda m: (0, 0)),
        ],
        out_specs=(pl.BlockSpec((bm, d), lambda m: (m, 0)),
                   pl.BlockSpec((bm, LANES), lambda m: (m, 0)),
                   pl.BlockSpec((1, LANES), lambda m: (0, 0))),
        scratch_shapes=[pltpu.VMEM((bm, bm), BF16), pltpu.VMEM((1, LANES), F32),
                        pltpu.VMEM((2 * bm, d), BF16)],
        compiler_params=_params("arbitrary"),
        name="moe_router",
    )(x2, norm_gain, scale, shift, w_hl)


def _dispatch_kernel(zend_ref, znum_ref, s1_ref, s2_ref, h_ref, xs_ref, zero_scr, sem, zsem):
    rows = h_ref.shape[0]

    @pl.when(pl.program_id(0) == 0)
    def _():
        zero_scr[...] = jnp.zeros(zero_scr.shape, zero_scr.dtype)

        for e in range(zend_ref.shape[0]):
            def fill(k, e=e):
                start = pl.multiple_of(zend_ref[e] - (k + 1) * ZERO_ROWS, ZERO_ROWS)
                return pltpu.make_async_copy(zero_scr, xs_ref.at[pl.ds(start, ZERO_ROWS)], zsem)

            def start_fill(k, carry, fill=fill):
                fill(k).start()
                return carry

            def wait_fill(k, carry, fill=fill):
                fill(k).wait()
                return carry

            lax.fori_loop(0, znum_ref[e], start_fill, 0)
            lax.fori_loop(0, znum_ref[e], wait_fill, 0)

    def row_copy(i, slot_ref):
        return pltpu.make_async_copy(h_ref.at[pl.ds(i, 1)], xs_ref.at[pl.ds(slot_ref[0, i], 1)], sem)

    def body(i, carry):
        row_copy(i, s1_ref).start(priority=0)
        row_copy(i, s2_ref).start(priority=1)
        return carry

    lax.fori_loop(0, rows, body, 0, unroll=8)
    for _ in range(TOP_K):
        pltpu.make_async_copy(h_ref, xs_ref.at[pl.ds(0, rows)], sem).wait()


def _dispatch(h, slot1, slot2, zero_end, zero_chunks, n_slots):
    t, d = h.shape
    bm = ROW_BLOCK
    nblk = t // bm
    slots = [s.reshape(nblk, 1, bm) for s in (slot1, slot2)]
    smem_spec = pl.BlockSpec((None, 1, bm), lambda m, zs, zn: (m, 0, 0), memory_space=pltpu.SMEM)
    grid_spec = pltpu.PrefetchScalarGridSpec(
        num_scalar_prefetch=2,
        grid=(nblk,),
        in_specs=[smem_spec, smem_spec, pl.BlockSpec((bm, d), lambda m, zs, zn: (m, 0))],
        out_specs=pl.BlockSpec(memory_space=pl.ANY),
        scratch_shapes=[pltpu.VMEM((ZERO_ROWS, d), F32), pltpu.SemaphoreType.DMA(()),
                        pltpu.SemaphoreType.DMA(())],
    )
    return pl.pallas_call(
        _dispatch_kernel,
        out_shape=jax.ShapeDtypeStruct((n_slots, d), F32),
        grid_spec=grid_spec,
        compiler_params=_params("arbitrary"),
        name="moe_dispatch",
    )(zero_end, zero_chunks, slots[0], slots[1], h)


def _filled_quarters(rows_ref):
    shift = MOE_QUARTER.bit_length() - 1
    return lax.shift_right_logical(rows_ref[pl.program_id(0)] + (MOE_QUARTER - 1), shift)


def _moe_up_kernel(be_ref, nl_ref, rows_ref, xs_ref, wg_ref, wu_ref, o_ref, h_scr):
    del be_ref, nl_ref
    quarters = _filled_quarters(rows_ref)
    chunk = MOE_CHUNK

    @pl.when(jnp.logical_and(quarters > 0, pl.program_id(1) == 0))
    def _():
        def body(c, carry):
            r0 = pl.multiple_of(c * chunk, chunk)
            h_scr[pl.ds(r0, chunk), :] = xs_ref[pl.ds(r0, chunk), :].astype(h_scr.dtype)
            return carry

        lax.fori_loop(0, quarters * (MOE_QUARTER // chunk), body, 0)

    for q in range(MOE_BLOCK // MOE_QUARTER + 1):
        rows = q * MOE_QUARTER

        @pl.when(quarters == q)
        def _(rows=rows):
            if rows:
                _swiglu_tile(h_scr[0:rows, :], wg_ref, wu_ref, o_ref, rows)
            if rows < MOE_BLOCK:
                o_ref[rows:, :] = jnp.zeros((MOE_BLOCK - rows, o_ref.shape[1]), o_ref.dtype)


def _moe_up(xs, w_gate_up, li, blk_expert, n_live, blk_rows):
    d = xs.shape[1]
    n_blocks = blk_expert.shape[0]
    f = w_gate_up.shape[-1] // 2
    bm, bn = MOE_BLOCK, 512
    nt = f // bn

    def row(b, nl):
        return jnp.minimum(b, nl[0] - 1)

    def col(b, j, nl):
        return jnp.where(b < nl[0], j, nt - 1)

    grid_spec = pltpu.PrefetchScalarGridSpec(
        num_scalar_prefetch=3,
        grid=(n_blocks, nt),
        in_specs=[
            pl.BlockSpec((bm, d), lambda b, j, be, nl, br: (row(b, nl), 0)),
            pl.BlockSpec((None, None, d, bn),
                         lambda b, j, be, nl, br: (li, be[row(b, nl)], 0, col(b, j, nl))),
            pl.BlockSpec((None, None, d, bn),
                         lambda b, j, be, nl, br: (li, be[row(b, nl)], 0, nt + col(b, j, nl))),
        ],
        out_specs=pl.BlockSpec((bm, bn), lambda b, j, be, nl, br: (b, j)),
        scratch_shapes=[pltpu.VMEM((bm, d), BF16)],
    )
    return pl.pallas_call(
        _moe_up_kernel,
        out_shape=jax.ShapeDtypeStruct((n_blocks * bm, f), BF16),
        grid_spec=grid_spec,
        compiler_params=_params("arbitrary", "arbitrary"),
        name="moe_gate_up",
    )(blk_expert, n_live, blk_rows, xs, w_gate_up, w_gate_up)


def _moe_down_kernel(be_ref, nl_ref, rows_ref, a_ref, w_ref, o_ref):
    del be_ref, nl_ref
    quarters = _filled_quarters(rows_ref)

    for q in range(MOE_BLOCK // MOE_QUARTER + 1):
        rows = q * MOE_QUARTER

        @pl.when(quarters == q)
        def _(rows=rows):
            if rows:
                o_ref[0:rows, :] = jnp.dot(a_ref[0:rows, :], w_ref[...].astype(BF16),
                                           preferred_element_type=F32)
            if rows < MOE_BLOCK:
                o_ref[rows:, :] = jnp.zeros((MOE_BLOCK - rows, o_ref.shape[1]), o_ref.dtype)


def _moe_down(a, w_down, li, blk_expert, n_live, blk_rows):
    s, f = a.shape
    d = w_down.shape[-1]
    bm, bn = MOE_BLOCK, 256
    nt = d // bn

    def row(b, nl):
        return jnp.minimum(b, nl[0] - 1)

    def col(b, j, nl):
        return jnp.where(b < nl[0], j, nt - 1)

    grid_spec = pltpu.PrefetchScalarGridSpec(
        num_scalar_prefetch=3,
        grid=(s // bm, nt),
        in_specs=[
            pl.BlockSpec((bm, f), lambda b, j, be, nl, br: (row(b, nl), 0)),
            pl.BlockSpec((None, None, f, bn),
                         lambda b, j, be, nl, br: (li, be[row(b, nl)], 0, col(b, j, nl))),
        ],
        out_specs=pl.BlockSpec((bm, bn), lambda b, j, be, nl, br: (b, j)),
    )
    return pl.pallas_call(
        _moe_down_kernel,
        out_shape=jax.ShapeDtypeStruct((s, d), F32),
        grid_spec=grid_spec,
        compiler_params=_params("arbitrary", "arbitrary"),
        name="moe_down",
    )(blk_expert, n_live, blk_rows, a, w_down)


def _combine_kernel(s1_ref, s2_ref, n1_ref, n2_ref, x_ref, gate_ref, meta_ref, ys_ref, o_ref, gbuf, sem):
    m = pl.program_id(0)
    rows = x_ref.shape[0]
    chunk = 128

    def issue(par, first_ref, second_ref):
        def row_copy(i, slot_ref, k):
            return pltpu.make_async_copy(ys_ref.at[pl.ds(slot_ref[0, i], 1)],
                                         gbuf.at[par, k, pl.ds(i, 1)], sem.at[par, k])

        def body(g, carry):
            r0 = pl.multiple_of(g * 8, 8)
            for k in range(8):
                row_copy(r0 + k, first_ref, 0).start(priority=0)
                row_copy(r0 + k, second_ref, 1).start(priority=1)
            return carry

        lax.fori_loop(0, rows // 8, body, 0)

    def step(par):
        @pl.when(m == 0)
        def _():
            issue(par, s1_ref, s2_ref)

        @pl.when(m + 1 < pl.num_programs(0))
        def _():
            issue(1 - par, n1_ref, n2_ref)

        for k in range(TOP_K):
            pltpu.make_async_copy(ys_ref.at[pl.ds(0, rows)], gbuf.at[par, k], sem.at[par, k]).wait()

        gate = gate_ref[...]

        def mix(c, carry):
            sl = pl.ds(pl.multiple_of(c * chunk, chunk), chunk)
            w1 = meta_ref[sl, 4:5]
            w2 = meta_ref[sl, 5:6]
            o_ref[sl, :] = x_ref[sl, :] + gate * (w1 * gbuf[par, 0, sl, :] + w2 * gbuf[par, 1, sl, :])
            return carry

        lax.fori_loop(0, rows // chunk, mix, 0)

    for par in range(2):
        pl.when(m % 2 == par)(functools.partial(step, par))


def _combine(ys, slot1, slot2, meta, x2, gate, seq):
    t, d = x2.shape
    bm = ROUTE_BLOCK
    nblk = t // bm
    per_seq = seq // bm
    slots = [s.reshape(nblk, 1, bm) for s in (slot1, slot2)]
    smem_spec = pl.BlockSpec((None, 1, bm), lambda m: (m, 0, 0), memory_space=pltpu.SMEM)
    next_spec = pl.BlockSpec((None, 1, bm), lambda m: (jnp.minimum(m + 1, nblk - 1), 0, 0),
                             memory_space=pltpu.SMEM)
    return pl.pallas_call(
        _combine_kernel,
        out_shape=jax.ShapeDtypeStruct((t, d), F32),
        grid=(nblk,),
        in_specs=[smem_spec, smem_spec, next_spec, next_spec,
                  pl.BlockSpec((bm, d), lambda m: (m, 0)),
                  pl.BlockSpec((None, 1, d), lambda m: (m // per_seq, 0, 0)),
                  pl.BlockSpec((bm, LANES), lambda m: (m, 0)),
                  pl.BlockSpec(memory_space=pl.ANY)],
        out_specs=pl.BlockSpec((bm, d), lambda m: (m, 0)),
        scratch_shapes=[pltpu.VMEM((2, TOP_K, bm, d), F32), pltpu.SemaphoreType.DMA((2, TOP_K))],
        compiler_params=_params("arbitrary"),
        name="moe_combine",
    )(slots[0], slots[1], slots[0], slots[1], x2, gate, meta, ys)


def _moe(x2, norm_gain, scale, shift, gate, w_router, w_gate_up, w_down, layer, li, seq):
    t, d = x2.shape
    e = w_router.shape[-1]
    blk = MOE_BLOCK
    h, meta, counts = _router(x2, norm_gain, scale, shift, w_router[li], layer, seq)

    cnt = counts[0, :e].astype(jnp.int32)
    nblk = (cnt + blk - 1) // blk
    blk_end = jnp.cumsum(nblk)
    blk_start = blk_end - nblk
    off = blk_start * blk
    n_blocks = pl.cdiv(TOP_K * t, blk) + e
    n_live = blk_end[-1:].astype(jnp.int32)
    block_ids = jnp.arange(n_blocks, dtype=jnp.int32)
    blk_expert = jnp.minimum(jnp.sum(block_ids[:, None] >= blk_end[None, :], axis=1), e - 1).astype(jnp.int32)
    blk_rows = jnp.clip(cnt[blk_expert] - (block_ids - blk_start[blk_expert]) * blk, 0, blk)
    blk_rows = jnp.where(block_ids < n_live[0], blk_rows, 0).astype(jnp.int32)
    ids = meta[:, 0:2].astype(jnp.int32)
    ranks = meta[:, 2:4].astype(jnp.int32)
    slots = off[ids] + ranks
    slot1, slot2 = slots[:, 0], slots[:, 1]

    n_slots = n_blocks * blk
    zero_end = jnp.concatenate([off[1:], jnp.full((1,), n_slots, jnp.int32)]).astype(jnp.int32)
    zero_chunks = ((zero_end - (off + cnt) + ZERO_ROWS - 1) // ZERO_ROWS).astype(jnp.int32)
    xs = _dispatch(h, slot1, slot2, zero_end, zero_chunks, n_slots)
    a = _moe_up(xs, w_gate_up, li, blk_expert, n_live, blk_rows)
    ys = _moe_down(a, w_down, li, blk_expert, n_live, blk_rows)
    return _combine(ys, slot1, slot2, meta, x2, gate, seq)


def kernel(x, c, ada_w, ada_b, norm_mix, w_in, q_norm, k_norm, attn_sinks, conv_w, w_attn_branch,
           w_conv_branch, w_out, norm_ffn, ffn_w_gate_up, ffn_w_down, moe_w_router, moe_w_gate_up,
           moe_w_down):
    batch, seq, d = x.shape
    depth = ada_w.shape[0]
    d_attn = w_attn_branch.shape[1]
    d_conv = w_conv_branch.shape[1]
    n_heads = d_attn // HEAD_DIM
    d_kv = (n_heads // GQA_GROUP) * HEAD_DIM
    col_b = d_attn + 2 * d_kv
    col_ga = col_b + 3 * d_conv
    col_gc = col_ga + d
    assert w_in.shape[-1] == col_gc + d and seq % ROW_BLOCK == 0
    assert 2 * HEAD_DIM == LANES and GQA_GROUP == 4 and (n_heads // GQA_GROUP) % 2 == 0

    n_mod = ada_w.shape[-1] // d
    mod = _adaln(c, ada_w, ada_b).reshape(depth, batch, n_mod, 1, d)
    bias = _alibi_bias(n_heads)
    x2 = x.reshape(batch * seq, d)
    w_attn_bf16 = _to_bf16(w_attn_branch, 1, 8)
    w_conv_bf16 = _to_bf16(w_conv_branch, 1, 8)

    for l in range(depth):
        shift_m, scale_m, gate_m, shift_f, scale_f, gate_f = (mod[l, :, i] for i in range(n_mod))

        reps_q, reps_k = d_attn // HEAD_DIM, d_kv // HEAD_DIM
        head_gain = jnp.concatenate([jnp.tile(q_norm[l] * HEAD_DIM ** -0.5, reps_q),
                                     jnp.tile(k_norm[l], reps_k), jnp.ones((d_kv,), F32)])[None]
        head_mask = jnp.concatenate([jnp.ones((d_attn + d_kv,), F32), jnp.zeros((d_kv,), F32)])[None]

        h_mix = _modnorm(x2, norm_mix.reshape(depth, 1, d), scale_m, shift_m, l, seq)
        y = _inproj(h_mix, w_in, l, head_gain, head_mask)
        attn = _attention(y, attn_sinks[l], bias, batch, seq, d_attn, d_kv)
        merged = _branch_merge(attn, y, conv_w, w_attn_bf16, w_conv_bf16, l, seq, col_b, col_ga, col_gc)
        x2 = _resid_proj(merged, w_out, l, x2, gate_m, seq, 2 * ROW_BLOCK, 512, "mixer_out_proj")

        norm_f = norm_ffn.reshape(depth, 1, d)
        if l % 2 == 0:
            a = _ffn_up(x2, norm_f, scale_f, shift_f, ffn_w_gate_up, l, l // 2, seq)
            x2 = _resid_proj(a, _to_bf16(ffn_w_down, 4, 8), l // 2, x2, gate_f, seq, ROW_BLOCK, 512, "ffn_down")
        else:
            x2 = _moe(x2, norm_f, scale_f, shift_f, gate_f, moe_w_router, moe_w_gate_up, moe_w_down,
                      l, l // 2, seq)
    return x2.reshape(batch, seq, d)
```

```python
import functools

import jax
import jax.numpy as jnp
from jax import lax
from jax.experimental import pallas as pl
from jax.experimental.pallas import tpu as pltpu

F32 = jnp.float32
BF16 = jnp.bfloat16

HEAD_DIM = 64
GQA_GROUP = 4
ATTN_BLOCK = 128
TOP_K = 2
EPS = 1e-6

LANES = 128
BF16_ROWS = 16
V7X_VMEM_BYTES = 64 * 1024 * 1024
VMEM_LIMIT = V7X_VMEM_BYTES - 8 * 1024 * 1024

ROW_BLOCK = 1024
ROUTE_BLOCK = 512
MOE_BLOCK = ROW_BLOCK
MOE_QUARTER = MOE_BLOCK // 4
MOE_CHUNK = 128
ZERO_ROWS = 256


def _params(*sem):
    return pltpu.CompilerParams(dimension_semantics=sem, vmem_limit_bytes=VMEM_LIMIT)


def _sigmoid(x):
    return 0.5 + 0.5 * jnp.tanh(0.5 * x)


def _silu(x):
    half = 0.5 * x
    return half + half * jnp.tanh(half)


MXU_COLS = 256


def _swiglu_tile(h, wg_ref, wu_ref, o_ref, rows):
    for c0 in range(0, o_ref.shape[1], MXU_COLS):
        cols = slice(c0, c0 + MXU_COLS)
        g = jnp.dot(h, wg_ref[:, cols].astype(BF16), preferred_element_type=F32)
        u = jnp.dot(h, wu_ref[:, cols].astype(BF16), preferred_element_type=F32)
        o_ref[0:rows, cols] = (_silu(g) * u).astype(o_ref.dtype)


def _modulated_norm(x, gain_scale, shift):
    ms = jnp.mean(x * x, axis=-1, keepdims=True)
    return (x * lax.rsqrt(ms + EPS)) * gain_scale + shift


def _fill_modulated_norm(x_ref, gain_ref, scale_ref, shift_ref, h_ref, chunk=128):
    gain_scale = gain_ref[...] * (1.0 + scale_ref[...])
    shift = shift_ref[...]

    def body(c, carry):
        r0 = pl.multiple_of(c * chunk, chunk)
        h = _modulated_norm(x_ref[pl.ds(r0, chunk), :], gain_scale, shift)
        h_ref[pl.ds(r0, chunk), :] = h.astype(h_ref.dtype)
        return carry

    lax.fori_loop(0, x_ref.shape[0] // chunk, body, 0)


def _cast_kernel(x_ref, o_ref, *, chunk):
    def body(c, carry):
        sl = pl.ds(pl.multiple_of(c * chunk, chunk), chunk)
        o_ref[sl, :] = x_ref[sl, :].astype(o_ref.dtype)
        return carry

    lax.fori_loop(0, x_ref.shape[0] // chunk, body, 0)


def _to_bf16(w, row_blocks, chunks):
    n, r, c = w.shape
    br = r // row_blocks
    assert br * row_blocks == r and br % (chunks * BF16_ROWS) == 0
    spec = pl.BlockSpec((None, br, c), lambda i, a: (i, a, 0))
    return pl.pallas_call(
        functools.partial(_cast_kernel, chunk=br // chunks),
        out_shape=jax.ShapeDtypeStruct(w.shape, BF16),
        grid=(n, row_blocks),
        in_specs=[spec],
        out_specs=spec,
        compiler_params=_params("arbitrary", "arbitrary"),
        name="weight_to_bf16",
    )(w)


def _adaln_kernel(c_ref, w_ref, b_ref, o_ref):
    s = _silu(c_ref[...]).astype(BF16)
    o_ref[...] = jnp.dot(s, w_ref[...].astype(BF16), preferred_element_type=F32) + b_ref[...]


def _adaln(c, ada_w, ada_b):
    depth, d, n = ada_w.shape
    b = c.shape[0]
    rows = 8 * pl.cdiv(b, 8)
    c_pad = jnp.pad(c, ((0, rows - b), (0, 0)))
    bn = 1024
    out = pl.pallas_call(
        _adaln_kernel,
        out_shape=jax.ShapeDtypeStruct((depth, rows, n), F32),
        grid=(depth, n // bn),
        in_specs=[
            pl.BlockSpec((rows, d), lambda l, j: (0, 0)),
            pl.BlockSpec((None, d, bn), lambda l, j: (l, 0, j)),
            pl.BlockSpec((None, 1, bn), lambda l, j: (l, 0, j)),
        ],
        out_specs=pl.BlockSpec((None, rows, bn), lambda l, j: (l, 0, j)),
        compiler_params=_params("arbitrary", "arbitrary"),
        name="adaln_mod",
    )(c_pad, ada_w, ada_b.reshape(depth, 1, n))
    return out[:, :b]


def _modnorm_kernel(x_ref, gain_ref, scale_ref, shift_ref, o_ref):
    _fill_modulated_norm(x_ref, gain_ref, scale_ref, shift_ref, o_ref)


def _modnorm(x2, norm_gain, scale, shift, layer, seq):
    t, d = x2.shape
    bm = ROUTE_BLOCK
    per_seq = seq // bm
    return pl.pallas_call(
        _modnorm_kernel,
        out_shape=jax.ShapeDtypeStruct((t, d), BF16),
        grid=(t // bm,),
        in_specs=[
            pl.BlockSpec((bm, d), lambda m: (m, 0)),
            pl.BlockSpec((None, 1, d), lambda m: (layer, 0, 0)),
            pl.BlockSpec((None, 1, d), lambda m: (m // per_seq, 0, 0)),
            pl.BlockSpec((None, 1, d), lambda m: (m // per_seq, 0, 0)),
        ],
        out_specs=pl.BlockSpec((bm, d), lambda m: (m, 0)),
        compiler_params=_params("arbitrary"),
        name="mixer_modnorm",
    )(x2, norm_gain, scale, shift)


def _inproj_kernel(h_ref, w_ref, hg_ref, hm_ref, o_ref, w_scr, *, norm_tiles):
    n = pl.program_id(0)

    @pl.when(pl.program_id(1) == 0)
    def _():
        w_scr[...] = w_ref[...].astype(BF16)

    acc = jnp.dot(h_ref[...], w_scr[...], preferred_element_type=F32)

    @pl.when(n < norm_tiles)
    def _():
        bn = acc.shape[1]
        shift = HEAD_DIM.bit_length() - 1
        r = lax.shift_right_logical(lax.broadcasted_iota(jnp.int32, (bn, bn), 0), shift)
        c = lax.shift_right_logical(lax.broadcasted_iota(jnp.int32, (bn, bn), 1), shift)
        head_mean = jnp.where(r == c, 1.0 / HEAD_DIM, 0.0).astype(BF16)
        ms = jnp.dot((acc * acc).astype(BF16), head_mean, preferred_element_type=F32)
        normed = acc * lax.rsqrt(ms + EPS) * hg_ref[...]
        o_ref[...] = jnp.where(hm_ref[...] > 0.0, normed, acc).astype(o_ref.dtype)

    @pl.when(n >= norm_tiles)
    def _():
        o_ref[...] = acc.astype(o_ref.dtype)


def _inproj(h, w_in, layer, head_gain, head_mask):
    t, d = h.shape
    n = w_in.shape[-1]
    bm, bn = 2 * ROW_BLOCK, 512
    norm_cols = head_gain.shape[-1]
    kern = functools.partial(_inproj_kernel, norm_tiles=norm_cols // bn)
    return pl.pallas_call(
        kern,
        out_shape=jax.ShapeDtypeStruct((t, n), BF16),
        grid=(n // bn, t // bm),
        in_specs=[
            pl.BlockSpec((bm, d), lambda j, m: (m, 0)),
            pl.BlockSpec((None, d, bn), lambda j, m: (layer, 0, j)),
            pl.BlockSpec((1, bn), lambda j, m: (0, jnp.minimum(j, norm_cols // bn - 1))),
            pl.BlockSpec((1, bn), lambda j, m: (0, jnp.minimum(j, norm_cols // bn - 1))),
        ],
        out_specs=pl.BlockSpec((bm, bn), lambda j, m: (m, j)),
        scratch_shapes=[pltpu.VMEM((d, bn), BF16)],
        compiler_params=_params("arbitrary", "arbitrary"),
        name="mixer_in_proj",
    )(h, w_in, head_gain, head_mask)


ATTN_STEP_BLOCKS = 2


def _attn_kernel(sinks_ref, q_ref, kvp_ref, kvc_ref, bias_ref, o_ref, *, n_kv):
    blk = ATTN_BLOCK
    first = pl.program_id(1) == 0
    dkv = n_kv * HEAD_DIM
    q = q_ref[...].astype(F32)
    kv = jnp.concatenate([kvp_ref[...], kvc_ref[...]], axis=0).astype(F32)
    k_all = kv[:, :dkv]
    v_all = kv[:, dkv:]
    lo = lax.broadcasted_iota(jnp.int32, (1, LANES), 1) < HEAD_DIM
    col = lax.broadcasted_iota(jnp.int32, (1, 2 * blk), 1)
    no_prev = jnp.where(jnp.logical_and(first, col < blk), -jnp.inf, 0.0).astype(F32)
    first_head = lax.broadcasted_iota(jnp.int32, (4 * blk, LANES), 0) < 2 * blk
    lo_full = lax.broadcasted_iota(jnp.int32, (4 * blk, LANES), 1) < HEAD_DIM
    sum_cols = jnp.where(first_head == lo_full, 1.0, 0.0).astype(BF16)

    for j in range(n_kv):
        t, half = divmod(j, 2)
        kp = k_all[:, LANES * t:LANES * (t + 1)]
        vp = v_all[:, LANES * t:LANES * (t + 1)]
        kr = pltpu.roll(kp, HEAD_DIM, 1)
        vr = pltpu.roll(vp, HEAD_DIM, 1)
        k_lo = jnp.where(lo, kr if half else kp, 0.0).astype(BF16)
        v_lo = jnp.where(lo, vr if half else vp, 0.0).astype(BF16)
        v_hi = jnp.where(lo, 0.0, vp if half else vr).astype(BF16)
        base = GQA_GROUP * HEAD_DIM * j
        for sb in range(ATTN_STEP_BLOCKS):
            rows = slice(blk * sb, blk * (sb + 1))
            keys = slice(blk * sb, blk * (sb + 2))
            v_stack = jnp.concatenate(
                [jnp.concatenate([v_lo[keys], v_hi[keys]], axis=0), sum_cols], axis=1)
            qa = q[rows, base:base + LANES]
            qb = q[rows, base + LANES:base + 2 * LANES]
            q_stack = jnp.concatenate(
                [qa, pltpu.roll(qa, HEAD_DIM, 1), qb, pltpu.roll(qb, HEAD_DIM, 1)], axis=0).astype(BF16)
            s = lax.dot_general(q_stack, k_lo[keys], (((1,), (1,)), ((), ())),
                                preferred_element_type=F32)
            probs, sink_w = [], []
            for g in range(GQA_GROUP):
                h = GQA_GROUP * j + g
                bias = bias_ref[h] + no_prev if sb == 0 else bias_ref[h]
                sg = s[blk * g:blk * (g + 1)] + bias
                sink = sinks_ref[h]
                m = jnp.maximum(jnp.max(sg, axis=-1, keepdims=True), sink)
                probs.append(jnp.exp(sg - m).astype(BF16))
                sink_w.append(jnp.exp(sink - m))
            pp = jnp.concatenate(
                [jnp.concatenate([probs[2 * pair], probs[2 * pair + 1]], axis=1)
                 for pair in range(GQA_GROUP // 2)], axis=0)
            o_all = jnp.dot(pp, v_stack, preferred_element_type=F32)
            for pair in range(GQA_GROUP // 2):
                o = o_all[blk * pair:blk * (pair + 1)]
                denom = o[:, LANES:] + jnp.where(lo, sink_w[2 * pair], sink_w[2 * pair + 1])
                o_ref[rows, base + LANES * pair:base + LANES * (pair + 1)] = (
                    o[:, :LANES] / denom).astype(o_ref.dtype)


def _alibi_bias(n_heads):
    blk = ATTN_BLOCK
    slopes = 2.0 ** (-8.0 * jnp.arange(1, n_heads + 1, dtype=F32) / n_heads)
    dist = (jnp.arange(blk) + blk)[:, None] - jnp.arange(2 * blk)[None, :]
    valid = (dist >= 0) & (dist < blk)
    bias = -(slopes[:, None, None] * dist.astype(F32)[None])
    return jnp.where(valid[None], bias, -jnp.inf)


def _attention(y, sinks, bias, batch, seq, d_attn, d_kv):
    t = y.shape[0]
    blk = ATTN_BLOCK
    step = ATTN_STEP_BLOCKS * blk
    ns = seq // step
    kv_w = 2 * d_kv
    kv_col = d_attn // kv_w
    kern = functools.partial(_attn_kernel, n_kv=d_kv // HEAD_DIM)
    grid_spec = pltpu.PrefetchScalarGridSpec(
        num_scalar_prefetch=1,
        grid=(batch, ns),
        in_specs=[
            pl.BlockSpec((step, d_attn), lambda b, i, s: (b * ns + i, 0)),
            pl.BlockSpec((blk, kv_w),
                         lambda b, i, s: ((b * ns + i) * ATTN_STEP_BLOCKS - jnp.minimum(i, 1), kv_col)),
            pl.BlockSpec((step, kv_w), lambda b, i, s: (b * ns + i, kv_col)),
            pl.BlockSpec(bias.shape, lambda b, i, s: (0, 0, 0)),
        ],
        out_specs=pl.BlockSpec((step, d_attn), lambda b, i, s: (b * ns + i, 0)),
    )
    return pl.pallas_call(
        kern,
        out_shape=jax.ShapeDtypeStruct((t, d_attn), BF16),
        grid_spec=grid_spec,
        compiler_params=_params("arbitrary", "arbitrary"),
        name="swa_attention",
    )(sinks, y, y, y, bias)


HALO = BF16_ROWS


GATE_COLS = 512


def _branch_kernel(attn_ref, bc_ref, cx_ref, hbc_ref, hcx_ref, cw_ref, wa_ref, wc_ref, *rest,
                   blocks_per_seq, taps):
    n_gate = (len(rest) - 3) // 2
    ga_refs, gc_refs = rest[:n_gate], rest[n_gate:2 * n_gate]
    o_ref, u_scr, conv_scr = rest[2 * n_gate:]
    m = pl.program_id(0)
    n = pl.program_id(1)
    dc = conv_scr.shape[1]
    half = dc // 2
    chunk = 64

    @pl.when(n == 0)
    def _():
        keep = jnp.where(m % blocks_per_seq == 0, 0.0, 1.0).astype(F32)
        hc = jnp.concatenate([hbc_ref[:, dc:], hcx_ref[:, :half]], axis=1).astype(F32)
        hx = hcx_ref[:, half:].astype(F32)
        u_scr[0:HALO, :] = hc * hx * keep

        def fill_u(c, carry):
            r0 = pl.multiple_of(c * chunk, chunk)
            cc = jnp.concatenate([bc_ref[pl.ds(r0, chunk), dc:], cx_ref[pl.ds(r0, chunk), :half]], axis=1)
            xx = cx_ref[pl.ds(r0, chunk), half:]
            u_scr[pl.ds(HALO + r0, chunk), :] = cc.astype(F32) * xx.astype(F32)
            return carry

        lax.fori_loop(0, conv_scr.shape[0] // chunk, fill_u, 0)

        def conv_rows(c, carry):
            r0 = pl.multiple_of(c * chunk, chunk)
            w0 = pl.multiple_of(r0 + HALO - 8, 8)
            win = u_scr[pl.ds(w0, chunk + 8), :]
            acc = cw_ref[taps - 1:taps, :] * win[8:]
            for back in range(1, taps):
                acc = acc + cw_ref[taps - 1 - back:taps - back, :] * pltpu.roll(win, back, 0)[8:]
            b = bc_ref[pl.ds(r0, chunk), :dc].astype(F32)
            conv_scr[pl.ds(r0, chunk), :] = (b * acc).astype(conv_scr.dtype)
            return carry

        lax.fori_loop(0, conv_scr.shape[0] // chunk, conv_rows, 0)

    a = jnp.dot(attn_ref[...], wa_ref[...].astype(BF16), preferred_element_type=F32)
    c = jnp.dot(conv_scr[...], wc_ref[...].astype(BF16), preferred_element_type=F32)
    for i, (ga_ref, gc_ref) in enumerate(zip(ga_refs, gc_refs)):
        cols = slice(i * GATE_COLS, (i + 1) * GATE_COLS)
        merged = (_sigmoid(ga_ref[...].astype(F32)) * a[:, cols]
                  + _sigmoid(gc_ref[...].astype(F32)) * c[:, cols])
        o_ref[:, cols] = merged.astype(o_ref.dtype)


def _branch_merge(attn, y, conv_w, w_attn, w_conv, layer, seq, col_b, col_ga, col_gc):
    t, d_attn = attn.shape
    dc = w_conv.shape[1]
    d = w_attn.shape[-1]
    taps = conv_w.shape[1]
    bm, bn = ROW_BLOCK, 1024
    wide = dc + dc // 2
    assert col_b % wide == 0 and taps - 1 <= 8
    assert col_ga % GATE_COLS == 0 and col_gc % GATE_COLS == 0 and bn % GATE_COLS == 0
    gates = bn // GATE_COLS
    cb = col_b // wide
    halo_blocks = bm // HALO
    kern = functools.partial(_branch_kernel, blocks_per_seq=seq // bm, taps=taps)

    def halo_map(col):
        return lambda m, j: (jnp.maximum(m * halo_blocks - 1, 0), col)

    return pl.pallas_call(
        kern,
        out_shape=jax.ShapeDtypeStruct((t, d), BF16),
        grid=(t // bm, d // bn),
        in_specs=[
            pl.BlockSpec((bm, d_attn), lambda m, j: (m, 0)),
            pl.BlockSpec((bm, wide), lambda m, j: (m, cb)),
            pl.BlockSpec((bm, wide), lambda m, j: (m, cb + 1)),
            pl.BlockSpec((HALO, wide), halo_map(cb)),
            pl.BlockSpec((HALO, wide), halo_map(cb + 1)),
            pl.BlockSpec((None, taps, dc), lambda m, j: (layer, 0, 0)),
            pl.BlockSpec((None, d_attn, bn), lambda m, j: (layer, 0, j)),
            pl.BlockSpec((None, dc, bn), lambda m, j: (layer, 0, j)),
        ] + [
            pl.BlockSpec((bm, GATE_COLS), lambda m, j, c0=c0, i=i: (m, c0 // GATE_COLS + gates * j + i))
            for c0 in (col_ga, col_gc) for i in range(gates)
        ],
        out_specs=pl.BlockSpec((bm, bn), lambda m, j: (m, j)),
        scratch_shapes=[pltpu.VMEM((bm + HALO, dc), F32), pltpu.VMEM((bm, dc), BF16)],
        compiler_params=_params("arbitrary", "arbitrary"),
        name="branch_merge",
    )(attn, y, y, y, y, conv_w, w_attn, w_conv, *([y] * (2 * gates)))


def _resid_proj_kernel(lhs_ref, w_ref, x_ref, gate_ref, o_ref):
    acc = jnp.dot(lhs_ref[...], w_ref[...].astype(BF16), preferred_element_type=F32)
    o_ref[...] = x_ref[...] + gate_ref[...] * acc


def _resid_proj(lhs, w, layer, x2, gate, seq, bm, bn, name):
    t, k = lhs.shape
    d = w.shape[-1]
    per_seq = seq // bm
    return pl.pallas_call(
        _resid_proj_kernel,
        out_shape=jax.ShapeDtypeStruct((t, d), F32),
        grid=(t // bm, d // bn),
        in_specs=[
            pl.BlockSpec((bm, k), lambda m, j: (m, 0)),
            pl.BlockSpec((None, k, bn), lambda m, j: (layer, 0, j)),
            pl.BlockSpec((bm, bn), lambda m, j: (m, j)),
            pl.BlockSpec((None, 1, bn), lambda m, j: (m // per_seq, 0, j)),
        ],
        out_specs=pl.BlockSpec((bm, bn), lambda m, j: (m, j)),
        compiler_params=_params("arbitrary", "arbitrary"),
        name=name,
    )(lhs, w, x2, gate)


def _ffn_up_kernel(x_ref, gain_ref, scale_ref, shift_ref, wg_ref, wu_ref, o_ref, h_scr):
    @pl.when(pl.program_id(1) == 0)
    def _():
        _fill_modulated_norm(x_ref, gain_ref, scale_ref, shift_ref, h_scr)

    _swiglu_tile(h_scr[...], wg_ref, wu_ref, o_ref, o_ref.shape[0])


def _ffn_up(x2, norm_gain, scale, shift, w_gate_up, layer, li, seq):
    t, d = x2.shape
    f = w_gate_up.shape[-1] // 2
    bm, bn = ROW_BLOCK, 512
    per_seq = seq // bm
    return pl.pallas_call(
        _ffn_up_kernel,
        out_shape=jax.ShapeDtypeStruct((t, f), BF16),
        grid=(t // bm, f // bn),
        in_specs=[
            pl.BlockSpec((bm, d), lambda m, j: (m, 0)),
            pl.BlockSpec((None, 1, d), lambda m, j: (layer, 0, 0)),
            pl.BlockSpec((None, 1, d), lambda m, j: (m // per_seq, 0, 0)),
            pl.BlockSpec((None, 1, d), lambda m, j: (m // per_seq, 0, 0)),
            pl.BlockSpec((None, d, bn), lambda m, j: (li, 0, j)),
            pl.BlockSpec((None, d, bn), lambda m, j: (li, 0, f // bn + j)),
        ],
        out_specs=pl.BlockSpec((bm, bn), lambda m, j: (m, j)),
        scratch_shapes=[pltpu.VMEM((bm, d), BF16)],
        compiler_params=_params("arbitrary", "arbitrary"),
        name="ffn_gate_up",
    )(x2, norm_gain, scale, shift, w_gate_up, w_gate_up)


def _router_kernel(x_ref, gain_ref, scale_ref, shift_ref, rw_ref, h_ref, meta_ref, cnt_ref,
                   tri_scr, base_scr, hl_scr, *, n_experts):
    i = pl.program_id(0)
    rows = x_ref.shape[0]
    chunk = 64

    @pl.when(i == 0)
    def _():
        r = lax.broadcasted_iota(jnp.int32, (rows, rows), 0)
        c = lax.broadcasted_iota(jnp.int32, (rows, rows), 1)
        tri_scr[...] = jnp.where(c < r, 1.0, 0.0).astype(BF16)
        base_scr[...] = jnp.zeros(base_scr.shape, F32)

    gain_scale = gain_ref[...] * (1.0 + scale_ref[...])
    shift = shift_ref[...]

    def body(c, carry):
        r0 = pl.multiple_of(c * chunk, chunk)
        h = _modulated_norm(x_ref[pl.ds(r0, chunk), :], gain_scale, shift)
        h_ref[pl.ds(r0, chunk), :] = h
        hi = h.astype(BF16)
        hl_scr[pl.ds(r0, chunk), :] = hi
        hl_scr[pl.ds(pl.multiple_of(rows + r0, chunk), chunk), :] = (h - hi.astype(F32)).astype(BF16)
        return carry

    lax.fori_loop(0, rows // chunk, body, 0)

    prod = jnp.dot(hl_scr[...], rw_ref[...], preferred_element_type=F32)
    lg = (prod[:rows, :LANES] + prod[:rows, LANES:]) + (prod[rows:, :LANES] + prod[rows:, LANES:])
    lane_i = lax.broadcasted_iota(jnp.int32, lg.shape, 1)
    lg = jnp.where(lane_i < n_experts, lg, -jnp.inf)
    lane = lane_i.astype(F32)
    m1 = jnp.max(lg, axis=-1, keepdims=True)
    i1 = jnp.min(jnp.where(lg == m1, lane, float(LANES)), axis=-1, keepdims=True)
    lg2 = jnp.where(lane == i1, -jnp.inf, lg)
    m2 = jnp.max(lg2, axis=-1, keepdims=True)
    i2 = jnp.min(jnp.where(lg2 == m2, lane, float(LANES)), axis=-1, keepdims=True)
    ex = jnp.exp(m2 - m1)
    w1 = 1.0 / (1.0 + ex)
    w2 = ex / (1.0 + ex)
    onehot = jnp.where(jnp.logical_or(lane == i1, lane == i2), 1.0, 0.0)
    ranks = jnp.dot(tri_scr[...], onehot.astype(BF16), preferred_element_type=F32) + base_scr[...]
    r1 = jnp.sum(jnp.where(lane == i1, ranks, 0.0), axis=-1, keepdims=True)
    r2 = jnp.sum(jnp.where(lane == i2, ranks, 0.0), axis=-1, keepdims=True)
    meta = jnp.zeros(lg.shape, F32)
    for pos, val in enumerate((i1, i2, r1, r2, w1, w2)):
        meta = jnp.where(lane == float(pos), val, meta)
    meta_ref[...] = meta
    new_base = base_scr[...] + jnp.sum(onehot, axis=0, keepdims=True)
    base_scr[...] = new_base
    cnt_ref[...] = new_base


def _router(x2, norm_gain, scale, shift, w_router, layer, seq):
    t, d = x2.shape
    e = w_router.shape[-1]
    bm = ROUTE_BLOCK
    per_seq = seq // bm
    kern = functools.partial(_router_kernel, n_experts=e)
    w_pad = jnp.pad(w_router, ((0, 0), (0, LANES - e)))
    w_hi = w_pad.astype(BF16)
    w_lo = (w_pad - w_hi.astype(F32)).astype(BF16)
    w_hl = jnp.concatenate([w_hi, w_lo], axis=1)
    return pl.pallas_call(
        kern,
        out_shape=(jax.ShapeDtypeStruct((t, d), F32),
                   jax.ShapeDtypeStruct((t, LANES), F32),
                   jax.ShapeDtypeStruct((1, LANES), F32)),
        grid=(t // bm,),
        in_specs=[
            pl.BlockSpec((bm, d), lambda m: (m, 0)),
            pl.BlockSpec((None, 1, d), lambda m: (layer, 0, 0)),
            pl.BlockSpec((None, 1, d), lambda m: (m // per_seq, 0, 0)),
            pl.BlockSpec((None, 1, d), lambda m: (m // per_seq, 0, 0)),
            pl.BlockSpec((d, 2 * LANES), lambda m: (0, 0)),
        ],
        out_specs=(pl.BlockSpec((bm, d), lambda m: (m, 0)),
                   pl.BlockSpec((bm, LANES), lambda m: (m, 0)),
                   pl.BlockSpec((1, LANES), lambda m: (0, 0))),
        scratch_shapes=[pltpu.VMEM((bm, bm), BF16), pltpu.VMEM((1, LANES), F32),
                        pltpu.VMEM((2 * bm, d), BF16)],
        compiler_params=_params("arbitrary"),
        name="moe_router",
    )(x2, norm_gain, scale, shift, w_hl)


def _dispatch_kernel(zend_ref, znum_ref, s1_ref, s2_ref, h_ref, xs_ref, zero_scr, sem, zsem):
    rows = h_ref.shape[0]

    @pl.when(pl.program_id(0) == 0)
    def _():
        zero_scr[...] = jnp.zeros(zero_scr.shape, zero_scr.dtype)

        for e in range(zend_ref.shape[0]):
            def fill(k, e=e):
                start = pl.multiple_of(zend_ref[e] - (k + 1) * ZERO_ROWS, ZERO_ROWS)
                return pltpu.make_async_copy(zero_scr, xs_ref.at[pl.ds(start, ZERO_ROWS)], zsem)

            def start_fill(k, carry, fill=fill):
                fill(k).start()
                return carry

            def wait_fill(k, carry, fill=fill):
                fill(k).wait()
                return carry

            lax.fori_loop(0, znum_ref[e], start_fill, 0)
            lax.fori_loop(0, znum_ref[e], wait_fill, 0)

    def row_copy(i, slot_ref):
        return pltpu.make_async_copy(h_ref.at[pl.ds(i, 1)], xs_ref.at[pl.ds(slot_ref[0, i], 1)], sem)

    def body(i, carry):
        row_copy(i, s1_ref).start(priority=0)
        row_copy(i, s2_ref).start(priority=1)
        return carry

    lax.fori_loop(0, rows, body, 0, unroll=8)
    for _ in range(TOP_K):
        pltpu.make_async_copy(h_ref, xs_ref.at[pl.ds(0, rows)], sem).wait()


def _dispatch(h, slot1, slot2, zero_end, zero_chunks, n_slots):
    t, d = h.shape
    bm = ROW_BLOCK
    nblk = t // bm
    slots = [s.reshape(nblk, 1, bm) for s in (slot1, slot2)]
    smem_spec = pl.BlockSpec((None, 1, bm), lambda m, zs, zn: (m, 0, 0), memory_space=pltpu.SMEM)
    grid_spec = pltpu.PrefetchScalarGridSpec(
        num_scalar_prefetch=2,
        grid=(nblk,),
        in_specs=[smem_spec, smem_spec, pl.BlockSpec((bm, d), lambda m, zs, zn: (m, 0))],
        out_specs=pl.BlockSpec(memory_space=pl.ANY),
        scratch_shapes=[pltpu.VMEM((ZERO_ROWS, d), F32), pltpu.SemaphoreType.DMA(()),
                        pltpu.SemaphoreType.DMA(())],
    )
    return pl.pallas_call(
        _dispatch_kernel,
        out_shape=jax.ShapeDtypeStruct((n_slots, d), F32),
        grid_spec=grid_spec,
        compiler_params=_params("arbitrary"),
        name="moe_dispatch",
    )(zero_end, zero_chunks, slots[0], slots[1], h)


def _filled_quarters(rows_ref):
    shift = MOE_QUARTER.bit_length() - 1
    return lax.shift_right_logical(rows_ref[pl.program_id(0)] + (MOE_QUARTER - 1), shift)


def _moe_up_kernel(be_ref, nl_ref, rows_ref, xs_ref, wg_ref, wu_ref, o_ref, h_scr):
    del be_ref, nl_ref
    quarters = _filled_quarters(rows_ref)
    chunk = MOE_CHUNK

    @pl.when(jnp.logical_and(quarters > 0, pl.program_id(1) == 0))
    def _():
        def body(c, carry):
            r0 = pl.multiple_of(c * chunk, chunk)
            h_scr[pl.ds(r0, chunk), :] = xs_ref[pl.ds(r0, chunk), :].astype(h_scr.dtype)
            return carry

        lax.fori_loop(0, quarters * (MOE_QUARTER // chunk), body, 0)

    for q in range(MOE_BLOCK // MOE_QUARTER + 1):
        rows = q * MOE_QUARTER

        @pl.when(quarters == q)
        def _(rows=rows):
            if rows:
                _swiglu_tile(h_scr[0:rows, :], wg_ref, wu_ref, o_ref, rows)
            if rows < MOE_BLOCK:
                o_ref[rows:, :] = jnp.zeros((MOE_BLOCK - rows, o_ref.shape[1]), o_ref.dtype)


def _moe_up(xs, w_gate_up, li, blk_expert, n_live, blk_rows):
    d = xs.shape[1]
    n_blocks = blk_expert.shape[0]
    f = w_gate_up.shape[-1] // 2
    bm, bn = MOE_BLOCK, 512
    nt = f // bn

    def row(b, nl):
        return jnp.minimum(b, nl[0] - 1)

    def col(b, j, nl):
        return jnp.where(b < nl[0], j, nt - 1)

    grid_spec = pltpu.PrefetchScalarGridSpec(
        num_scalar_prefetch=3,
        grid=(n_blocks, nt),
        in_specs=[
            pl.BlockSpec((bm, d), lambda b, j, be, nl, br: (row(b, nl), 0)),
            pl.BlockSpec((None, None, d, bn),
                         lambda b, j, be, nl, br: (li, be[row(b, nl)], 0, col(b, j, nl))),
            pl.BlockSpec((None, None, d, bn),
                         lambda b, j, be, nl, br: (li, be[row(b, nl)], 0, nt + col(b, j, nl))),
        ],
        out_specs=pl.BlockSpec((bm, bn), lambda b, j, be, nl, br: (b, j)),
        scratch_shapes=[pltpu.VMEM((bm, d), BF16)],
    )
    return pl.pallas_call(
        _moe_up_kernel,
        out_shape=jax.ShapeDtypeStruct((n_blocks * bm, f), BF16),
        grid_spec=grid_spec,
        compiler_params=_params("arbitrary", "arbitrary"),
        name="moe_gate_up",
    )(blk_expert, n_live, blk_rows, xs, w_gate_up, w_gate_up)


def _moe_down_kernel(be_ref, nl_ref, rows_ref, a_ref, w_ref, o_ref):
    del be_ref, nl_ref
    quarters = _filled_quarters(rows_ref)

    for q in range(MOE_BLOCK // MOE_QUARTER + 1):
        rows = q * MOE_QUARTER

        @pl.when(quarters == q)
        def _(rows=rows):
            if rows:
                o_ref[0:rows, :] = jnp.dot(a_ref[0:rows, :], w_ref[...].astype(BF16),
                                           preferred_element_type=F32)
            if rows < MOE_BLOCK:
                o_ref[rows:, :] = jnp.zeros((MOE_BLOCK - rows, o_ref.shape[1]), o_ref.dtype)


def _moe_down(a, w_down, li, blk_expert, n_live, blk_rows):
    s, f = a.shape
    d = w_down.shape[-1]
    bm, bn = MOE_BLOCK, 256
    nt = d // bn

    def row(b, nl):
        return jnp.minimum(b, nl[0] - 1)

    def col(b, j, nl):
        return jnp.where(b < nl[0], j, nt - 1)

    grid_spec = pltpu.PrefetchScalarGridSpec(
        num_scalar_prefetch=3,
        grid=(s // bm, nt),
        in_specs=[
            pl.BlockSpec((bm, f), lambda b, j, be, nl, br: (row(b, nl), 0)),
            pl.BlockSpec((None, None, f, bn),
                         lambda b, j, be, nl, br: (li, be[row(b, nl)], 0, col(b, j, nl))),
        ],
        out_specs=pl.BlockSpec((bm, bn), lambda b, j, be, nl, br: (b, j)),
    )
    return pl.pallas_call(
        _moe_down_kernel,
        out_shape=jax.ShapeDtypeStruct((s, d), F32),
        grid_spec=grid_spec,
        compiler_params=_params("arbitrary", "arbitrary"),
        name="moe_down",
    )(blk_expert, n_live, blk_rows, a, w_down)


def _combine_kernel(s1_ref, s2_ref, n1_ref, n2_ref, x_ref, gate_ref, meta_ref, ys_ref, o_ref, gbuf, sem):
    m = pl.program_id(0)
    rows = x_ref.shape[0]
    chunk = 128

    def issue(par, first_ref, second_ref):
        def row_copy(i, slot_ref, k):
            return pltpu.make_async_copy(ys_ref.at[pl.ds(slot_ref[0, i], 1)],
                                         gbuf.at[par, k, pl.ds(i, 1)], sem.at[par, k])

        def body(g, carry):
            r0 = pl.multiple_of(g * 8, 8)
            for k in range(8):
                row_copy(r0 + k, first_ref, 0).start(priority=0)
                row_copy(r0 + k, second_ref, 1).start(priority=1)
            return carry

        lax.fori_loop(0, rows // 8, body, 0)

    def step(par):
        @pl.when(m == 0)
        def _():
            issue(par, s1_ref, s2_ref)

        @pl.when(m + 1 < pl.num_programs(0))
        def _():
            issue(1 - par, n1_ref, n2_ref)

        for k in range(TOP_K):
            pltpu.make_async_copy(ys_ref.at[pl.ds(0, rows)], gbuf.at[par, k], sem.at[par, k]).wait()

        gate = gate_ref[...]

        def mix(c, carry):
            sl = pl.ds(pl.multiple_of(c * chunk, chunk), chunk)
            w1 = meta_ref[sl, 4:5]
            w2 = meta_ref[sl, 5:6]
            o_ref[sl, :] = x_ref[sl, :] + gate * (w1 * gbuf[par, 0, sl, :] + w2 * gbuf[par, 1, sl, :])
            return carry

        lax.fori_loop(0, rows // chunk, mix, 0)

    for par in range(2):
        pl.when(m % 2 == par)(functools.partial(step, par))


def _combine(ys, slot1, slot2, meta, x2, gate, seq):
    t, d = x2.shape
    bm = ROUTE_BLOCK
    nblk = t // bm
    per_seq = seq // bm
    slots = [s.reshape(nblk, 1, bm) for s in (slot1, slot2)]
    smem_spec = pl.BlockSpec((None, 1, bm), lambda m: (m, 0, 0), memory_space=pltpu.SMEM)
    next_spec = pl.BlockSpec((None, 1, bm), lambda m: (jnp.minimum(m + 1, nblk - 1), 0, 0),
                             memory_space=pltpu.SMEM)
    return pl.pallas_call(
        _combine_kernel,
        out_shape=jax.ShapeDtypeStruct((t, d), F32),
        grid=(nblk,),
        in_specs=[smem_spec, smem_spec, next_spec, next_spec,
                  pl.BlockSpec((bm, d), lambda m: (m, 0)),
                  pl.BlockSpec((None, 1, d), lambda m: (m // per_seq, 0, 0)),
                  pl.BlockSpec((bm, LANES), lambda m: (m, 0)),
                  pl.BlockSpec(memory_space=pl.ANY)],
        out_specs=pl.BlockSpec((bm, d), lambda m: (m, 0)),
        scratch_shapes=[pltpu.VMEM((2, TOP_K, bm, d), F32), pltpu.SemaphoreType.DMA((2, TOP_K))],
        compiler_params=_params("arbitrary"),
        name="moe_combine",
    )(slots[0], slots[1], slots[0], slots[1], x2, gate, meta, ys)


def _moe(x2, norm_gain, scale, shift, gate, w_router, w_gate_up, w_down, layer, li, seq):
    t, d = x2.shape
    e = w_router.shape[-1]
    blk = MOE_BLOCK
    h, meta, counts = _router(x2, norm_gain, scale, shift, w_router[li], layer, seq)

    cnt = counts[0, :e].astype(jnp.int32)
    nblk = (cnt + blk - 1) // blk
    blk_end = jnp.cumsum(nblk)
    blk_start = blk_end - nblk
    off = blk_start * blk
    n_blocks = pl.cdiv(TOP_K * t, blk) + e
    n_live = blk_end[-1:].astype(jnp.int32)
    block_ids = jnp.arange(n_blocks, dtype=jnp.int32)
    blk_expert = jnp.minimum(jnp.sum(block_ids[:, None] >= blk_end[None, :], axis=1), e - 1).astype(jnp.int32)
    blk_rows = jnp.clip(cnt[blk_expert] - (block_ids - blk_start[blk_expert]) * blk, 0, blk)
    blk_rows = jnp.where(block_ids < n_live[0], blk_rows, 0).astype(jnp.int32)
    ids = meta[:, 0:2].astype(jnp.int32)
    ranks = meta[:, 2:4].astype(jnp.int32)
    slots = off[ids] + ranks
    slot1, slot2 = slots[:, 0], slots[:, 1]

    n_slots = n_blocks * blk
    zero_end = jnp.concatenate([off[1:], jnp.full((1,), n_slots, jnp.int32)]).astype(jnp.int32)
    zero_chunks = ((zero_end - (off + cnt) + ZERO_ROWS - 1) // ZERO_ROWS).astype(jnp.int32)
    xs = _dispatch(h, slot1, slot2, zero_end, zero_chunks, n_slots)
    a = _moe_up(xs, w_gate_up, li, blk_expert, n_live, blk_rows)
    ys = _moe_down(a, w_down, li, blk_expert, n_live, blk_rows)
    return _combine(ys, slot1, slot2, meta, x2, gate, seq)


def kernel(x, c, ada_w, ada_b, norm_mix, w_in, q_norm, k_norm, attn_sinks, conv_w, w_attn_branch,
           w_conv_branch, w_out, norm_ffn, ffn_w_gate_up, ffn_w_down, moe_w_router, moe_w_gate_up,
           moe_w_down):
    batch, seq, d = x.shape
    depth = ada_w.shape[0]
    d_attn = w_attn_branch.shape[1]
    d_conv = w_conv_branch.shape[1]
    n_heads = d_attn // HEAD_DIM
    d_kv = (n_heads // GQA_GROUP) * HEAD_DIM
    col_b = d_attn + 2 * d_kv
    col_ga = col_b + 3 * d_conv
    col_gc = col_ga + d
    assert w_in.shape[-1] == col_gc + d and seq % ROW_BLOCK == 0
    assert 2 * HEAD_DIM == LANES and GQA_GROUP == 4 and (n_heads // GQA_GROUP) % 2 == 0

    n_mod = ada_w.shape[-1] // d
    mod = _adaln(c, ada_w, ada_b).reshape(depth, batch, n_mod, 1, d)
    bias = _alibi_bias(n_heads)
    x2 = x.reshape(batch * seq, d)
    w_attn_bf16 = _to_bf16(w_attn_branch, 1, 8)
    w_conv_bf16 = _to_bf16(w_conv_branch, 1, 8)

    for l in range(depth):
        shift_m, scale_m, gate_m, shift_f, scale_f, gate_f = (mod[l, :, i] for i in range(n_mod))

        reps_q, reps_k = d_attn // HEAD_DIM, d_kv // HEAD_DIM
        head_gain = jnp.concatenate([jnp.tile(q_norm[l] * HEAD_DIM ** -0.5, reps_q),
                                     jnp.tile(k_norm[l], reps_k), jnp.ones((d_kv,), F32)])[None]
        head_mask = jnp.concatenate([jnp.ones((d_attn + d_kv,), F32), jnp.zeros((d_kv,), F32)])[None]

        h_mix = _modnorm(x2, norm_mix.reshape(depth, 1, d), scale_m, shift_m, l, seq)
        y = _inproj(h_mix, w_in, l, head_gain, head_mask)
        attn = _attention(y, attn_sinks[l], bias, batch, seq, d_attn, d_kv)
        merged = _branch_merge(attn, y, conv_w, w_attn_bf16, w_conv_bf16, l, seq, col_b, col_ga, col_gc)
        x2 = _resid_proj(merged, w_out, l, x2, gate_m, seq, 2 * ROW_BLOCK, 512, "mixer_out_proj")

        norm_f = norm_ffn.reshape(depth, 1, d)
        if l % 2 == 0:
            a = _ffn_up(x2, norm_f, scale_f, shift_f, ffn_w_gate_up, l, l // 2, seq)
            x2 = _resid_proj(a, _to_bf16(ffn_w_down, 4, 8), l // 2, x2, gate_f, seq, ROW_BLOCK, 512, "ffn_down")
        else:
            x2 = _moe(x2, norm_f, scale_f, shift_f, gate_f, moe_w_router, moe_w_gate_up, moe_w_down,
                      l, l // 2, seq)
    return x2.reshape(batch, seq, d)
```
